```python
import jax, jax.numpy as jnp
from jax import lax
import numpy as np

D_MODEL = 2048
BATCH = 2
SEQ = 4096
DEPTH = 2
DEC_BATCH = 8
DEC_SEQ = 8
PAST_LEN = 16384
PAGE_SIZE = 128

N_HEADS = 16
HEAD_DIM = D_MODEL // N_HEADS
D_FF = ((8 * D_MODEL // 3 + 255) // 256) * 256
CONV_WIDTH = 31
N_META = 16
Q_BLOCK = 128
N_A_LAYERS = DEPTH // 2
N_B_LAYERS = DEPTH - N_A_LAYERS
FFN_RESIDUAL = 0.5
RMS_EPS = 1e-6
LN_EPS = 1e-5
NEG_INF = -1e30

kernel_name = "yoco_conformer_conv_fox_decoder_step"


def rms_norm(x, g):
    xf = x.astype(jnp.float32)
    y = xf * lax.rsqrt(jnp.mean(xf * xf, axis=-1, keepdims=True) + RMS_EPS)
    return (y * g.astype(jnp.float32)).astype(x.dtype)


def layer_norm(x, g, b):
    xf = x.astype(jnp.float32)
    mu = jnp.mean(xf, axis=-1, keepdims=True)
    var = jnp.mean(jnp.square(xf - mu), axis=-1, keepdims=True)
    y = (xf - mu) * lax.rsqrt(var + LN_EPS)
    return (y * g.astype(jnp.float32) + b.astype(jnp.float32)).astype(x.dtype)


def swiglu(x, w1, w3, w2):
    return (jax.nn.silu(x @ w1) * (x @ w3)) @ w2


def conv_module(u, state, w_in, b_in, dw_w, dw_b, ln_g, ln_b, w_out, b_out):
    gl = u @ w_in + b_in
    a, g = jnp.split(gl, 2, axis=-1)
    glu = a * jax.nn.sigmoid(g)
    if state is None:
        state = jnp.zeros((glu.shape[0], CONV_WIDTH - 1, glu.shape[2]), glu.dtype)
    buf = jnp.concatenate([state.astype(glu.dtype), glu], axis=1)
    new_state = buf[:, -(CONV_WIDTH - 1):]
    y = lax.conv_general_dilated(buf, dw_w[:, None, :], window_strides=(1,), padding='VALID',
                                 dimension_numbers=('NWC', 'WIO', 'NWC'),
                                 feature_group_count=buf.shape[-1]) + dw_b
    y = jax.nn.silu(layer_norm(y, ln_g, ln_b))
    return y @ w_out + b_out, new_state


def fox_logits(q, k, c_q, c_k):
    s = jnp.einsum('bqhd,bkhd->bhqk', q, k, preferred_element_type=jnp.float32) * (HEAD_DIM ** -0.5)
    return s + jnp.transpose(c_q, (0, 2, 1))[:, :, :, None] - jnp.transpose(c_k, (0, 2, 1))[:, :, None, :]


def fox_prompt(q, k, v, logf):
    B, T, H, Dh = q.shape
    pad = (-T) % Q_BLOCK
    L = T + pad
    nb = L // Q_BLOCK
    c = jnp.cumsum(logf.astype(jnp.float32), axis=1)
    p4 = ((0, 0), (pad, 0), (0, 0), (0, 0))
    qp, kp, vp = jnp.pad(q, p4), jnp.pad(k, p4), jnp.pad(v, p4)
    cp = jnp.pad(c, ((0, 0), (pad, 0), (0, 0)))
    pos = jnp.arange(L)
    key_valid = pos >= pad
    qb = jnp.transpose(qp.reshape(B, nb, Q_BLOCK, H, Dh), (1, 0, 2, 3, 4))
    cqb = jnp.transpose(cp.reshape(B, nb, Q_BLOCK, H), (1, 0, 2, 3))

    def block(args):
        i, q_blk, cq_blk = args
        s = fox_logits(q_blk, kp, cq_blk, cp)
        qpos = i * Q_BLOCK + jnp.arange(Q_BLOCK)
        mask = (pos[None, :] <= qpos[:, None]) & key_valid[None, :]
        s = jnp.where(mask[None, None], s, NEG_INF)
        pr = jax.nn.softmax(s, axis=-1).astype(vp.dtype)
        return jnp.einsum('bhqk,bkhd->bqhd', pr, vp)

    out = lax.map(block, (jnp.arange(nb), qb, cqb))
    out = jnp.transpose(out, (1, 0, 2, 3, 4)).reshape(B, L, H, Dh)
    return out[:, pad:]


def fox_sample(q, k_new, v_new, logf_new, k_past, v_past, logf_past):
    S = q.shape[1]
    P = k_past.shape[1]
    c_past = jnp.cumsum(logf_past.astype(jnp.float32), axis=1)
    c_new = c_past[:, -1:, :] + jnp.cumsum(logf_new.astype(jnp.float32), axis=1)
    s_past = fox_logits(q, k_past, c_new, c_past)
    s_new = fox_logits(q, k_new, c_new, c_new)
    causal = jnp.tril(jnp.ones((S, S), dtype=bool))
    s_new = jnp.where(causal[None, None], s_new, NEG_INF)
    pr = jax.nn.softmax(jnp.concatenate([s_past, s_new], axis=-1), axis=-1).astype(v_new.dtype)
    return (jnp.einsum('bhqk,bkhd->bqhd', pr[..., :P], v_past)
            + jnp.einsum('bhqk,bkhd->bqhd', pr[..., P:], v_new))


def setup_inputs(seed: int = 0) -> dict:
    key = jax.random.key(seed)
    ks = jax.random.split(key, 40)
    f32 = jnp.float32
    n_pages = PAST_LEN // PAGE_SIZE
    n_used = DEC_BATCH * n_pages
    n_pool = n_used + max(1, n_used // 4)
    HD = N_HEADS * HEAD_DIM

    def nrm(k, shape, scale):
        return jax.random.normal(k, shape, f32) * scale

    def gain(k, shape):
        return 1.0 + nrm(k, shape, 0.01)

    page_table = jax.random.permutation(ks[0], n_pool)[:n_used].reshape(DEC_BATCH, n_pages).astype(jnp.int32)
    return {
        "x_prompt": nrm(ks[1], (BATCH, SEQ, D_MODEL), 1.0),
        "x_sample": nrm(ks[2], (DEC_BATCH, DEC_SEQ, D_MODEL), 1.0),
        "cache_k": nrm(ks[3], (n_pool, PAGE_SIZE, N_HEADS, HEAD_DIM), 1.0),
        "cache_v": nrm(ks[4], (n_pool, PAGE_SIZE, N_HEADS, HEAD_DIM), 1.0),
        "cache_logf": jax.nn.log_sigmoid(3.0 + nrm(ks[5], (n_pool, PAGE_SIZE, N_HEADS), 1.0)),
        "state_conv": nrm(ks[6], (N_A_LAYERS, DEC_BATCH, CONV_WIDTH - 1, D_MODEL), 1.0),
        "page_table": page_table,
        "meta_tokens": nrm(ks[7], (N_META, D_MODEL), 1.0),
        "norm_ffn1": gain(ks[8], (DEPTH, D_MODEL)),
        "ffn1_w1": nrm(ks[9], (DEPTH, D_MODEL, D_FF), D_MODEL ** -0.5),
        "ffn1_w3": nrm(ks[10], (DEPTH, D_MODEL, D_FF), D_MODEL ** -0.5),
        "ffn1_w2": nrm(ks[11], (DEPTH, D_FF, D_MODEL), D_FF ** -0.5),
        "norm_mix": gain(ks[12], (DEPTH, D_MODEL)),
        "conv_w_in": nrm(ks[13], (N_A_LAYERS, D_MODEL, 2 * D_MODEL), D_MODEL ** -0.5),
        "conv_b_in": nrm(ks[14], (N_A_LAYERS, 2 * D_MODEL), 0.01),
        "conv_dw_w": nrm(ks[15], (N_A_LAYERS, CONV_WIDTH, D_MODEL), CONV_WIDTH ** -0.5),
        "conv_dw_b": nrm(ks[16], (N_A_LAYERS, D_MODEL), 0.01),
        "conv_ln_g": gain(ks[17], (N_A_LAYERS, D_MODEL)),
        "conv_ln_b": nrm(ks[18], (N_A_LAYERS, D_MODEL), 0.01),
        "conv_w_out": nrm(ks[19], (N_A_LAYERS, D_MODEL, D_MODEL), D_MODEL ** -0.5),
        "conv_b_out": nrm(ks[20], (N_A_LAYERS, D_MODEL), 0.01),
        "norm_kv": gain(ks[21], (D_MODEL,)),
        "w_k": nrm(ks[22], (D_MODEL, HD), D_MODEL ** -0.5),
        "w_v": nrm(ks[23], (D_MODEL, HD), D_MODEL ** -0.5),
        "w_f": nrm(ks[24], (D_MODEL, N_HEADS), D_MODEL ** -0.5),
        "b_f": jnp.linspace(1.0, 6.0, N_HEADS, dtype=f32) + nrm(ks[25], (N_HEADS,), 0.01),
        "w_q": nrm(ks[26], (N_B_LAYERS, D_MODEL, HD), D_MODEL ** -0.5),
        "w_o": nrm(ks[27], (N_B_LAYERS, HD, D_MODEL), HD ** -0.5),
        "norm_ffn2": gain(ks[28], (DEPTH, D_MODEL)),
        "ffn2_w1": nrm(ks[29], (DEPTH, D_MODEL, D_FF), D_MODEL ** -0.5),
        "ffn2_w3": nrm(ks[30], (DEPTH, D_MODEL, D_FF), D_MODEL ** -0.5),
        "ffn2_w2": nrm(ks[31], (DEPTH, D_FF, D_MODEL), D_FF ** -0.5),
        "norm_final": gain(ks[32], (D_MODEL,)),
    }


def reference(x_prompt, x_sample, cache_k, cache_v, cache_logf, state_conv, page_table,
              meta_tokens, norm_ffn1, ffn1_w1, ffn1_w3, ffn1_w2, norm_mix,
              conv_w_in, conv_b_in, conv_dw_w, conv_dw_b, conv_ln_g, conv_ln_b, conv_w_out, conv_b_out,
              norm_kv, w_k, w_v, w_f, b_f, w_q, w_o,
              norm_ffn2, ffn2_w1, ffn2_w3, ffn2_w2, norm_final):

    def trunk(h, conv_state, past):
        B_, T_ = h.shape[0], h.shape[1]
        new_conv = []
        kv = None
        for layer in range(DEPTH):
            if layer == N_A_LAYERS:
                hk = rms_norm(h, norm_kv)
                k = (hk @ w_k).reshape(B_, T_, N_HEADS, HEAD_DIM)
                v = (hk @ w_v).reshape(B_, T_, N_HEADS, HEAD_DIM)
                logf = jax.nn.log_sigmoid((hk @ w_f + b_f).astype(jnp.float32))
                kv = (k, v, logf)
            h = h + FFN_RESIDUAL * swiglu(rms_norm(h, norm_ffn1[layer]), ffn1_w1[layer], ffn1_w3[layer], ffn1_w2[layer])
            u = rms_norm(h, norm_mix[layer])
            if layer < N_A_LAYERS:
                a = layer
                st = None if conv_state is None else conv_state[a]
                y, st_new = conv_module(u, st, conv_w_in[a], conv_b_in[a], conv_dw_w[a], conv_dw_b[a],
                                        conv_ln_g[a], conv_ln_b[a], conv_w_out[a], conv_b_out[a])
                new_conv.append(st_new)
            else:
                b = layer - N_A_LAYERS
                q = (u @ w_q[b]).reshape(B_, T_, N_HEADS, HEAD_DIM)
                k, v, logf = kv
                if past is None:
                    o = fox_prompt(q, k, v, logf)
                else:
                    o = fox_sample(q, k, v, logf, past[0], past[1], past[2])
                y = o.reshape(B_, T_, N_HEADS * HEAD_DIM) @ w_o[b]
            h = h + y
            h = h + FFN_RESIDUAL * swiglu(rms_norm(h, norm_ffn2[layer]), ffn2_w1[layer], ffn2_w3[layer], ffn2_w2[layer])
        return rms_norm(h, norm_final), jnp.stack(new_conv, axis=0), kv

    Bp = x_prompt.shape[0]
    meta = jnp.broadcast_to(meta_tokens[None].astype(x_prompt.dtype), (Bp, N_META, x_prompt.shape[2]))
    h0 = jnp.concatenate([meta, x_prompt], axis=1)
    yp, conv_prompt, kv_p = trunk(h0, None, None)
    y_prompt = yp[:, N_META:]
    k_prompt, v_prompt, logf_prompt = kv_p

    Bs = x_sample.shape[0]
    n_pages = page_table.shape[1]
    ps = cache_k.shape[1]
    k_past = cache_k[page_table].reshape(Bs, n_pages * ps, N_HEADS, HEAD_DIM)
    v_past = cache_v[page_table].reshape(Bs, n_pages * ps, N_HEADS, HEAD_DIM)
    logf_past = cache_logf[page_table].reshape(Bs, n_pages * ps, N_HEADS)
    y_sample, conv_sample, kv_s = trunk(x_sample, state_conv, (k_past, v_past, logf_past))
    k_sample, v_sample, logf_sample = kv_s

    return (y_prompt, y_sample, k_prompt, v_prompt, logf_prompt, conv_prompt,
            k_sample, v_sample, logf_sample, conv_sample)
```

```python
import functools

import jax
import jax.numpy as jnp
from jax import lax
from jax.experimental import pallas as pl
from jax.experimental.pallas import tpu as pltpu

F32 = jnp.float32
BF16 = jnp.bfloat16

RMS_EPS = 1e-6
LN_EPS = 1e-5
NEG_INF = -1e30
FFN_RESIDUAL = 0.5

LANES = 128
ROW_TILE = 1056
FFN_TILE = 256
PROJ_TILE = 512
KV_ROW_TILE = 768
ATT_TILE = 384
CONV_CHUNK = 264
CONV_LANES = 256
VMEM_LIMIT = 56 * 1024 * 1024


def _params(semantics):
    return pltpu.CompilerParams(dimension_semantics=semantics, vmem_limit_bytes=VMEM_LIMIT)


def _single(shape, index_map):
    return pl.BlockSpec(shape, index_map, pipeline_mode=pl.Buffered(1))


def _rms(x, g):
    ms = jnp.mean(x * x, axis=-1, keepdims=True)
    return x * lax.rsqrt(ms + RMS_EPS) * g


def _split3(x):
    hi = x.astype(BF16)
    r1 = x - hi.astype(F32)
    mid = r1.astype(BF16)
    lo = (r1 - mid.astype(F32)).astype(BF16)
    return hi, mid, lo


def _dot(a, b):
    return jnp.dot(a, b, preferred_element_type=F32)


def _dot_nt(a, b):
    return lax.dot_general(a, b, (((1,), (1,)), ((), ())), preferred_element_type=F32)


def _ffn_kernel(x_ref, g_ref, w1_ref, w3_ref, w2_ref, o_ref, xn_ref):
    @pl.when(pl.program_id(1) == 0)
    def _():
        x = x_ref[...]
        xn_ref[...] = _rms(x, g_ref[...]).astype(BF16)
        o_ref[...] = x

    xn = xn_ref[...]
    a = _dot(xn, w1_ref[...].astype(BF16))
    b = _dot(xn, w3_ref[...].astype(BF16))
    gate = (FFN_RESIDUAL * a * jax.nn.sigmoid(a) * b).astype(BF16)
    o_ref[...] += _dot(gate, w2_ref[...].astype(BF16))


def _ffn(h, g, w1, w3, w2):
    m, d = h.shape
    dff = w1.shape[1]
    return pl.pallas_call(
        _ffn_kernel,
        grid=(m // ROW_TILE, dff // FFN_TILE),
        in_specs=[
            _single((ROW_TILE, d), lambda i, f: (i, 0)),
            pl.BlockSpec((1, d), lambda i, f: (0, 0)),
            pl.BlockSpec((d, FFN_TILE), lambda i, f: (0, f)),
            pl.BlockSpec((d, FFN_TILE), lambda i, f: (0, f)),
            pl.BlockSpec((FFN_TILE, d), lambda i, f: (f, 0)),
        ],
        out_specs=pl.BlockSpec((ROW_TILE, d), lambda i, f: (i, 0)),
        out_shape=jax.ShapeDtypeStruct((m, d), F32),
        scratch_shapes=[pltpu.VMEM((ROW_TILE, d), BF16)],
        compiler_params=_params(("parallel", "arbitrary")),
        name="ffn",
    )(h, g.reshape(1, d), w1, w3, w2)


def _rms_proj_kernel(x_ref, g_ref, w_ref, o_ref, xn_ref, *, scale):
    @pl.when(pl.program_id(1) == 0)
    def _():
        xn_ref[...] = _rms(x_ref[...], g_ref[...]).astype(BF16)

    o_ref[...] = (_dot(xn_ref[...], w_ref[...].astype(BF16)) * scale).astype(o_ref.dtype)


def _rms_proj(h, g, w, scale, out_dtype):
    m, d = h.shape
    n = w.shape[1]
    return pl.pallas_call(
        functools.partial(_rms_proj_kernel, scale=scale),
        grid=(m // ROW_TILE, n // PROJ_TILE),
        in_specs=[
            _single((ROW_TILE, d), lambda i, j: (i, 0)),
            pl.BlockSpec((1, d), lambda i, j: (0, 0)),
            pl.BlockSpec((d, PROJ_TILE), lambda i, j: (0, j)),
        ],
        out_specs=pl.BlockSpec((ROW_TILE, PROJ_TILE), lambda i, j: (i, j)),
        out_shape=jax.ShapeDtypeStruct((m, n), out_dtype),
        scratch_shapes=[pltpu.VMEM((ROW_TILE, d), BF16)],
        compiler_params=_params(("parallel", "arbitrary")),
        name="rms_proj",
    )(h, g.reshape(1, d), w)


def _rms_glu_kernel(x_ref, g_ref, wa_ref, wg_ref, ba_ref, bg_ref, o_ref, xn_ref):
    @pl.when(pl.program_id(1) == 0)
    def _():
        xn_ref[...] = _rms(x_ref[...], g_ref[...]).astype(BF16)

    xn = xn_ref[...]
    a = _dot(xn, wa_ref[...].astype(BF16)) + ba_ref[...]
    gt = _dot(xn, wg_ref[...].astype(BF16)) + bg_ref[...]
    o_ref[...] = a * jax.nn.sigmoid(gt)


def _rms_glu(h, g, w_in, b_in):
    m, d = h.shape
    nt = d // PROJ_TILE
    return pl.pallas_call(
        _rms_glu_kernel,
        grid=(m // ROW_TILE, nt),
        in_specs=[
            _single((ROW_TILE, d), lambda i, j: (i, 0)),
            pl.BlockSpec((1, d), lambda i, j: (0, 0)),
            pl.BlockSpec((d, PROJ_TILE), lambda i, j: (0, j)),
            pl.BlockSpec((d, PROJ_TILE), lambda i, j: (0, j + nt)),
            pl.BlockSpec((1, PROJ_TILE), lambda i, j: (0, j)),
            pl.BlockSpec((1, PROJ_TILE), lambda i, j: (0, j + nt)),
        ],
        out_specs=pl.BlockSpec((ROW_TILE, PROJ_TILE), lambda i, j: (i, j)),
        out_shape=jax.ShapeDtypeStruct((m, d), F32),
        scratch_shapes=[pltpu.VMEM((ROW_TILE, d), BF16)],
        compiler_params=_params(("parallel", "arbitrary")),
        name="rms_glu",
    )(h, g.reshape(1, d), w_in, w_in, b_in.reshape(1, 2 * d), b_in.reshape(1, 2 * d))


def _log_sigmoid(z):
    return jnp.minimum(z, 0.0) - jnp.log1p(jnp.exp(-jnp.abs(z)))


def _kv_kernel(x_ref, g_ref, wk_ref, wv_ref, wft_ref, bf_ref,
               k32_ref, v32_ref, k16_ref, v16_ref, lft_ref, xn_ref):
    @pl.when(pl.program_id(1) == 0)
    def _():
        xn = _rms(x_ref[...], g_ref[...]).astype(BF16)
        xn_ref[...] = xn
        lft_ref[...] = _log_sigmoid(_dot_nt(wft_ref[...].astype(BF16), xn) + bf_ref[...])

    xn = xn_ref[...]
    k = _dot(xn, wk_ref[...].astype(BF16))
    v = _dot(xn, wv_ref[...].astype(BF16))
    k32_ref[...] = k
    v32_ref[...] = v
    k16_ref[...] = k.astype(BF16)
    v16_ref[...] = v.astype(BF16)


def _kv(h, g, w_k, w_v, w_f, b_f):
    m, d = h.shape
    n = w_k.shape[1]
    nh = w_f.shape[1]
    tile = pl.BlockSpec((KV_ROW_TILE, PROJ_TILE), lambda i, j: (i, j))
    wspec = pl.BlockSpec((d, PROJ_TILE), lambda i, j: (0, j))
    return pl.pallas_call(
        _kv_kernel,
        grid=(m // KV_ROW_TILE, n // PROJ_TILE),
        in_specs=[
            _single((KV_ROW_TILE, d), lambda i, j: (i, 0)),
            pl.BlockSpec((1, d), lambda i, j: (0, 0)),
            wspec, wspec,
            pl.BlockSpec((nh, d), lambda i, j: (0, 0)),
            pl.BlockSpec((nh, 1), lambda i, j: (0, 0)),
        ],
        out_specs=[tile, tile, tile, tile, pl.BlockSpec((nh, KV_ROW_TILE), lambda i, j: (0, i))],
        out_shape=[
            jax.ShapeDtypeStruct((m, n), F32), jax.ShapeDtypeStruct((m, n), F32),
            jax.ShapeDtypeStruct((m, n), BF16), jax.ShapeDtypeStruct((m, n), BF16),
            jax.ShapeDtypeStruct((nh, m), F32),
        ],
        scratch_shapes=[pltpu.VMEM((KV_ROW_TILE, d), BF16)],
        compiler_params=_params(("parallel", "arbitrary")),
        name="kv_proj",
    )(h, g.reshape(1, d), w_k, w_v, w_f.T, b_f.reshape(nh, 1))


def _ln_proj_res_kernel(x_ref, lg_ref, lb_ref, w_ref, b_ref, r_ref, o_ref, xn_ref):
    @pl.when(pl.program_id(1) == 0)
    def _():
        x = x_ref[...]
        mu = jnp.mean(x, axis=-1, keepdims=True)
        xc = x - mu
        var = jnp.mean(xc * xc, axis=-1, keepdims=True)
        y = xc * lax.rsqrt(var + LN_EPS) * lg_ref[...] + lb_ref[...]
        xn_ref[...] = (y * jax.nn.sigmoid(y)).astype(BF16)

    o_ref[...] = r_ref[...] + _dot(xn_ref[...], w_ref[...].astype(BF16)) + b_ref[...]


def _ln_proj_res(x, ln_g, ln_b, w, b, res):
    m, d = x.shape
    n = w.shape[1]
    tile = pl.BlockSpec((ROW_TILE, PROJ_TILE), lambda i, j: (i, j))
    return pl.pallas_call(
        _ln_proj_res_kernel,
        grid=(m // ROW_TILE, n // PROJ_TILE),
        in_specs=[
            _single((ROW_TILE, d), lambda i, j: (i, 0)),
            pl.BlockSpec((1, d), lambda i, j: (0, 0)),
            pl.BlockSpec((1, d), lambda i, j: (0, 0)),
            pl.BlockSpec((d, PROJ_TILE), lambda i, j: (0, j)),
            pl.BlockSpec((1, PROJ_TILE), lambda i, j: (0, j)),
            tile,
        ],
        out_specs=tile,
        out_shape=jax.ShapeDtypeStruct((m, n), F32),
        scratch_shapes=[pltpu.VMEM((ROW_TILE, d), BF16)],
        compiler_params=_params(("parallel", "arbitrary")),
        name="ln_proj_res",
    )(x, ln_g.reshape(1, d), ln_b.reshape(1, d), w, b.reshape(1, n), res)


def _proj_res_kernel(x_ref, w_ref, r_ref, o_ref):
    o_ref[...] = r_ref[...] + _dot(x_ref[...], w_ref[...].astype(BF16))


def _proj_res(x, w, res):
    m, d = x.shape
    n = w.shape[1]
    tile = pl.BlockSpec((ROW_TILE, PROJ_TILE), lambda i, j: (i, j))
    return pl.pallas_call(
        _proj_res_kernel,
        grid=(m // ROW_TILE, n // PROJ_TILE),
        in_specs=[
            pl.BlockSpec((ROW_TILE, d), lambda i, j: (i, 0)),
            pl.BlockSpec((d, PROJ_TILE), lambda i, j: (0, j)),
            tile,
        ],
        out_specs=tile,
        out_shape=jax.ShapeDtypeStruct((m, n), F32),
        compiler_params=_params(("parallel", "arbitrary")),
        name="proj_res",
    )(x, w, res)


def _rms_out_kernel(x_ref, g_ref, o_ref):
    o_ref[...] = _rms(x_ref[...], g_ref[...])


def _rms_out(h, g):
    m, d = h.shape
    tile = pl.BlockSpec((ROW_TILE, d), lambda i: (i, 0))
    return pl.pallas_call(
        _rms_out_kernel,
        grid=(m // ROW_TILE,),
        in_specs=[tile, pl.BlockSpec((1, d), lambda i: (0, 0))],
        out_specs=tile,
        out_shape=jax.ShapeDtypeStruct((m, d), F32),
        compiler_params=_params(("parallel",)),
        name="rms_out",
    )(h, g.reshape(1, d))


def _dwconv_kernel(x_ref, w_ref, b_ref, o_ref, buf_ref, *, width):
    t_len = x_ref.shape[1]
    lanes = x_ref.shape[2]
    lead = 33 - width
    buf_ref[pl.ds(0, 32), :] = jnp.zeros((32, lanes), F32)
    buf_ref[pl.ds(32, t_len), :] = x_ref[0]
    bias = b_ref[...]

    def chunk(c, carry):
        t0 = pl.multiple_of(c * CONV_CHUNK, 8)
        win = buf_ref[pl.ds(t0, CONV_CHUNK + 32), :]
        acc = jnp.broadcast_to(bias, (CONV_CHUNK, lanes))
        for r in range(8):
            taps = range(r, width, 8)
            z = win[r + lead:r + lead + CONV_CHUNK + 8 * (len(taps) - 1)]
            for a, j in enumerate(taps):
                acc = acc + w_ref[pl.ds(j, 1), :] * z[8 * a:8 * a + CONV_CHUNK]
        o_ref[0, pl.ds(t0, CONV_CHUNK), :] = acc
        return carry

    lax.fori_loop(0, t_len // CONV_CHUNK, chunk, 0)


def _dwconv(glu3, dw_w, dw_b):
    nb, t_len, d = glu3.shape
    width = dw_w.shape[0]
    blk = pl.BlockSpec((1, t_len, CONV_LANES), lambda b, c: (b, 0, c))
    return pl.pallas_call(
        functools.partial(_dwconv_kernel, width=width),
        grid=(nb, d // CONV_LANES),
        in_specs=[
            blk,
            pl.BlockSpec((width, CONV_LANES), lambda b, c: (0, c)),
            pl.BlockSpec((1, CONV_LANES), lambda b, c: (0, c)),
        ],
        out_specs=blk,
        out_shape=jax.ShapeDtypeStruct(glu3.shape, F32),
        scratch_shapes=[pltpu.VMEM((t_len + 32, CONV_LANES), F32)],
        compiler_params=_params(("parallel", "parallel")),
        name="dwconv",
    )(glu3, dw_w, dw_b.reshape(1, d))


def _dwconv_step_kernel(s_ref, x_ref, w_ref, b_ref, y_ref, ns_ref, buf_ref, *, width):
    n_state = width - 1
    n_new = x_ref.shape[1]
    d = x_ref.shape[2]
    buf_ref[pl.ds(0, n_state), :] = s_ref[0]
    buf_ref[pl.ds(n_state, n_new), :] = x_ref[0]
    acc = jnp.broadcast_to(b_ref[...], (n_new, d))
    for j in range(width):
        acc = acc + w_ref[pl.ds(j, 1), :] * buf_ref[pl.ds(j, n_new), :]
    y_ref[0] = acc
    ns_ref[0] = buf_ref[pl.ds(n_new, n_state), :]


def _dwconv_step(state, glu_s, dw_w, dw_b):
    nb, n_state, d = state.shape
    n_new = glu_s.shape[1]
    width = dw_w.shape[0]
    return pl.pallas_call(
        functools.partial(_dwconv_step_kernel, width=width),
        grid=(nb,),
        in_specs=[
            pl.BlockSpec((1, n_state, d), lambda b: (b, 0, 0)),
            pl.BlockSpec((1, n_new, d), lambda b: (b, 0, 0)),
            pl.BlockSpec((width, d), lambda b: (0, 0)),
            pl.BlockSpec((1, d), lambda b: (0, 0)),
        ],
        out_specs=[
            pl.BlockSpec((1, n_new, d), lambda b: (b, 0, 0)),
            pl.BlockSpec((1, n_state, d), lambda b: (b, 0, 0)),
        ],
        out_shape=[
            jax.ShapeDtypeStruct((nb, n_new, d), F32),
            jax.ShapeDtypeStruct((nb, n_state, d), F32),
        ],
        scratch_shapes=[pltpu.VMEM((n_state + n_new + 2, d), F32)],
        compiler_params=_params(("parallel",)),
        name="dwconv_step",
    )(state, glu_s, dw_w, dw_b.reshape(1, d))


def _neg_cumsum_kernel(lf_ref, o_ref, carry_ref):
    @pl.when(pl.program_id(1) == 0)
    def _():
        carry_ref[...] = jnp.zeros_like(carry_ref)

    row = lax.broadcasted_iota(jnp.int32, (LANES, LANES), 0)
    col = lax.broadcasted_iota(jnp.int32, (LANES, LANES), 1)
    upper = (row <= col).astype(BF16)
    hi, mid, lo = _split3(lf_ref[...])
    cs = (_dot(lo, upper) + _dot(mid, upper)) + _dot(hi, upper) + carry_ref[...]
    o_ref[...] = -cs
    carry_ref[...] = jnp.broadcast_to(cs[:, LANES - 1:LANES], cs.shape)


def _neg_cumsum(lft, nb):
    nh, m = lft.shape
    steps = m // nb // LANES
    blk = pl.BlockSpec((nh, LANES), lambda b, j: (0, b * steps + j))
    return pl.pallas_call(
        _neg_cumsum_kernel,
        grid=(nb, steps),
        in_specs=[blk],
        out_specs=blk,
        out_shape=jax.ShapeDtypeStruct((nh, m), F32),
        scratch_shapes=[pltpu.VMEM((nh, LANES), F32)],
        compiler_params=_params(("arbitrary", "arbitrary")),
        name="neg_cumsum",
    )(lft)


def _softmax_step(s, m_i, l_i, acc, v):
    m_new = jnp.maximum(m_i, jnp.max(s, axis=-1, keepdims=True))
    p = jnp.exp(s - m_new)
    alpha = jnp.exp(m_i - m_new)
    l_new = alpha * l_i + jnp.sum(p, axis=-1, keepdims=True)
    acc_new = alpha * acc + _dot(p.astype(BF16), v)
    return m_new, l_new, acc_new


def _attn_kernel(q_ref, k_ref, v_ref, nc_ref, o_ref):
    qi = pl.program_id(2)
    q = q_ref[...]
    tq, dh = q.shape

    def tile(kj, carry, diagonal):
        m_i, l_i, acc = carry
        start = pl.multiple_of(kj * ATT_TILE, ATT_TILE)
        k = k_ref[pl.ds(start, ATT_TILE), :]
        v = v_ref[pl.ds(start, ATT_TILE), :]
        s = _dot_nt(q, k) + nc_ref[kj]
        if diagonal:
            row = lax.broadcasted_iota(jnp.int32, s.shape, 0)
            col = lax.broadcasted_iota(jnp.int32, s.shape, 1)
            s = jnp.where(col <= row, s, NEG_INF)
        return _softmax_step(s, m_i, l_i, acc, v)

    init = (jnp.full((tq, 1), NEG_INF, F32), jnp.zeros((tq, 1), F32), jnp.zeros((tq, dh), F32))
    carry = lax.fori_loop(0, qi, lambda kj, c: tile(kj, c, False), init)
    _, l_i, acc = tile(qi, carry, True)
    o_ref[...] = (acc / l_i).astype(o_ref.dtype)


def _attention(q, k, v, negc, nb, nh):
    m, hd = q.shape
    dh = hd // nh
    t = m // nb
    nt = t // ATT_TILE
    negc4 = negc.reshape(nh, nb, nt, 1, ATT_TILE)
    return pl.pallas_call(
        _attn_kernel,
        grid=(nb, nh, nt),
        in_specs=[
            pl.BlockSpec((ATT_TILE, dh), lambda b, h, i: (b * nt + i, h)),
            pl.BlockSpec((t, dh), lambda b, h, i: (b, h)),
            pl.BlockSpec((t, dh), lambda b, h, i: (b, h)),
            pl.BlockSpec((None, None, nt, 1, ATT_TILE), lambda b, h, i: (h, b, 0, 0, 0)),
        ],
        out_specs=pl.BlockSpec((ATT_TILE, dh), lambda b, h, i: (b * nt + i, h)),
        out_shape=jax.ShapeDtypeStruct((m, hd), BF16),
        compiler_params=_params(("parallel", "parallel", "arbitrary")),
        name="fox_prompt",
    )(q, k, v, negc4)


def _paged_attn_kernel(pt_ref, q_ref, k_ref, v_ref, lf_ref, kn_ref, vn_ref, lfn_ref, o_ref,
                       qbd_ref, m_ref, l_ref, acc_ref, carry_ref, *, nh, n_new):
    p = pl.program_id(1)
    rows, hd = qbd_ref.shape
    dh = hd // nh
    page = k_ref.shape[1]
    q_shift = n_new.bit_length() - 1
    d_shift = dh.bit_length() - 1

    def head_mask():
        r = lax.broadcasted_iota(jnp.int32, (rows, hd), 0)
        c = lax.broadcasted_iota(jnp.int32, (rows, hd), 1)
        return (r >> q_shift) == (c >> d_shift)

    @pl.when(p == 0)
    def _():
        q = q_ref[0].astype(F32)
        qt = jnp.broadcast_to(q[None], (nh, n_new, hd)).reshape(rows, hd)
        qbd_ref[...] = jnp.where(head_mask(), qt, 0.0).astype(BF16)
        m_ref[...] = jnp.full_like(m_ref, NEG_INF)
        l_ref[...] = jnp.zeros_like(l_ref)
        acc_ref[...] = jnp.zeros_like(acc_ref)
        carry_ref[...] = jnp.zeros_like(carry_ref)

    r16 = lax.broadcasted_iota(jnp.int32, (rows, nh), 0)
    c16 = lax.broadcasted_iota(jnp.int32, (rows, nh), 1)
    expand = ((r16 >> q_shift) == c16).astype(BF16)
    kr = lax.broadcasted_iota(jnp.int32, (page, page), 0)
    kc = lax.broadcasted_iota(jnp.int32, (page, page), 1)
    lower = (kc <= kr).astype(BF16)

    def update(k, v, lf, causal):
        hi, mid, lo = _split3(lf)
        c = (_dot(lower, lo) + _dot(lower, mid)) + _dot(lower, hi) + carry_ref[...]
        hi, mid, lo = _split3(c)
        bias = (_dot_nt(expand, lo) + _dot_nt(expand, mid)) + _dot_nt(expand, hi)
        s = _dot_nt(qbd_ref[...], k) - bias
        if causal:
            qpos = lax.broadcasted_iota(jnp.int32, s.shape, 0) & (n_new - 1)
            kpos = lax.broadcasted_iota(jnp.int32, s.shape, 1)
            s = jnp.where(kpos <= qpos, s, NEG_INF)
        m_new, l_new, acc_new = _softmax_step(s, m_ref[...], l_ref[...], acc_ref[...], v)
        m_ref[...] = m_new
        l_ref[...] = l_new
        acc_ref[...] = acc_new
        carry_ref[...] = c[page - 1:page, :]

    update(k_ref[0].astype(BF16), v_ref[0].astype(BF16), lf_ref[0], False)

    @pl.when(p == pl.num_programs(1) - 1)
    def _():
        update(kn_ref[0], vn_ref[0], lfn_ref[0], True)
        o = jnp.where(head_mask(), acc_ref[...] / l_ref[...], 0.0)
        o_ref[0] = jnp.sum(o.reshape(nh, n_new, hd), axis=0).astype(o_ref.dtype)


def _paged_attention(page_table, q_s, cache_k, cache_v, cache_logf, k_new, v_new, lf_new, nh):
    nb, n_new, hd = q_s.shape
    page = cache_k.shape[1]
    n_pages = page_table.shape[1]
    rows = nh * n_new
    assert n_new & (n_new - 1) == 0 and (hd // nh) & (hd // nh - 1) == 0
    cache = pl.BlockSpec((1, page, hd), lambda b, p, pt: (pt[b * n_pages + p], 0, 0))
    per_b = lambda shape: pl.BlockSpec(shape, lambda b, p, pt: (b, 0, 0))
    grid_spec = pltpu.PrefetchScalarGridSpec(
        num_scalar_prefetch=1,
        grid=(nb, n_pages),
        in_specs=[
            per_b((1, n_new, hd)),
            cache, cache,
            pl.BlockSpec((1, page, nh), lambda b, p, pt: (pt[b * n_pages + p], 0, 0)),
            per_b((1, page, hd)), per_b((1, page, hd)), per_b((1, page, nh)),
        ],
        out_specs=per_b((1, n_new, hd)),
        scratch_shapes=[
            pltpu.VMEM((rows, hd), BF16),
            pltpu.VMEM((rows, 1), F32),
            pltpu.VMEM((rows, 1), F32),
            pltpu.VMEM((rows, hd), F32),
            pltpu.VMEM((1, nh), F32),
        ],
    )
    return pl.pallas_call(
        functools.partial(_paged_attn_kernel, nh=nh, n_new=n_new),
        grid_spec=grid_spec,
        out_shape=jax.ShapeDtypeStruct((nb, n_new, hd), BF16),
        compiler_params=_params(("parallel", "arbitrary")),
        name="fox_sample",
    )(page_table.reshape(-1), q_s, cache_k, cache_v, cache_logf, k_new, v_new, lf_new)


def kernel(x_prompt, x_sample, cache_k, cache_v, cache_logf, state_conv, page_table, meta_tokens, norm_ffn1, ffn1_w1, ffn1_w3, ffn1_w2, norm_mix, conv_w_in, conv_b_in, conv_dw_w, conv_dw_b, conv_ln_g, conv_ln_b, conv_w_out, conv_b_out, norm_kv, w_k, w_v, w_f, b_f, w_q, w_o, norm_ffn2, ffn2_w1, ffn2_w3, ffn2_w2, norm_final):
    nb, seq, d = x_prompt.shape
    ns, s_len, _ = x_sample.shape
    n_meta = meta_tokens.shape[0]
    nh = w_f.shape[1]
    dh = w_k.shape[1] // nh
    n_pool, page = cache_k.shape[0], cache_k.shape[1]
    width = conv_dw_w.shape[1]
    depth = norm_ffn1.shape[0]
    n_a = conv_w_in.shape[0]

    tp = seq + n_meta
    tpad = -(-(tp + ns * s_len) // ATT_TILE) * ATT_TILE
    m = nb * tpad
    s0 = (nb - 1) * tpad + tp
    n_s = ns * s_len
    assert m % ROW_TILE == 0 and m % KV_ROW_TILE == 0 and tpad % CONV_CHUNK == 0
    assert width <= 33 and d % CONV_LANES == 0
    assert s0 % 16 == 0 and tpad - tp >= n_s and depth == 2 * n_a == 2

    meta = meta_tokens.astype(F32)
    pieces = []
    for b in range(nb):
        pieces += [meta, x_prompt[b]]
        if b < nb - 1:
            pieces.append(jnp.zeros((tpad - tp, d), F32))
    pieces += [x_sample.reshape(n_s, d), jnp.zeros((tpad - tp - n_s, d), F32)]
    h = jnp.concatenate(pieces, axis=0)

    def prompt_rows(x):
        return x.reshape(nb, tpad, x.shape[-1])[:, :tp]

    def sample_rows(x):
        return x[s0:s0 + n_s]

    h = _ffn(h, norm_ffn1[0], ffn1_w1[0], ffn1_w3[0], ffn1_w2[0])
    glu = _rms_glu(h, norm_mix[0], conv_w_in[0], conv_b_in[0])
    y = _dwconv(glu.reshape(nb, tpad, d), conv_dw_w[0], conv_dw_b[0]).reshape(m, d)
    y_s, conv_s = _dwconv_step(state_conv[0], sample_rows(glu).reshape(ns, s_len, d),
                               conv_dw_w[0], conv_dw_b[0])
    y = lax.dynamic_update_slice(y, y_s.reshape(n_s, d), (s0, 0))
    conv_prompt = prompt_rows(glu)[None, :, tp - (width - 1):]
    conv_sample = conv_s[None]
    h = _ln_proj_res(y, conv_ln_g[0], conv_ln_b[0], conv_w_out[0], conv_b_out[0], h)
    h = _ffn(h, norm_ffn2[0], ffn2_w1[0], ffn2_w3[0], ffn2_w2[0])

    k32, v32, k16, v16, lft = _kv(h, norm_kv, w_k, w_v, w_f, b_f)
    h = _ffn(h, norm_ffn1[1], ffn1_w1[1], ffn1_w3[1], ffn1_w2[1])
    q = _rms_proj(h, norm_mix[1], w_q[0], dh ** -0.5, BF16)
    negc = _neg_cumsum(lft, nb)
    o = _attention(q, k16, v16, negc, nb, nh)

    def new_page(x):
        x = x.reshape(ns, s_len, x.shape[-1])
        return jnp.pad(x, ((0, 0), (0, page - s_len), (0, 0)))

    lf_rows = lft.T
    o_s = _paged_attention(
        page_table, sample_rows(q).reshape(ns, s_len, nh * dh),
        cache_k.reshape(n_pool, page, nh * dh), cache_v.reshape(n_pool, page, nh * dh), cache_logf,
        new_page(sample_rows(k16)), new_page(sample_rows(v16)), new_page(sample_rows(lf_rows)), nh)
    o = lax.dynamic_update_slice(o, o_s.reshape(n_s, nh * dh), (s0, 0))
    h = _proj_res(o, w_o[0], h)
    h = _ffn(h, norm_ffn2[1], ffn2_w1[1], ffn2_w3[1], ffn2_w2[1])
    yf = _rms_out(h, norm_final)

    y_prompt = prompt_rows(yf)[:, n_meta:]
    y_sample = sample_rows(yf).reshape(ns, s_len, d)
    k_prompt = prompt_rows(k32).reshape(nb, tp, nh, dh)
    v_prompt = prompt_rows(v32).reshape(nb, tp, nh, dh)
    logf_prompt = prompt_rows(lf_rows)
    k_sample = sample_rows(k32).reshape(ns, s_len, nh, dh)
    v_sample = sample_rows(v32).reshape(ns, s_len, nh, dh)
    logf_sample = sample_rows(lf_rows).reshape(ns, s_len, nh)
    return (y_prompt, y_sample, k_prompt, v_prompt, logf_prompt, conv_prompt,
            k_sample, v_sample, logf_sample, conv_sample)
```

```python
import functools

import jax
import jax.numpy as jnp
from jax import lax
from jax.experimental import pallas as pl
from jax.experimental.pallas import tpu as pltpu

F32 = jnp.float32
BF16 = jnp.bfloat16

RMS_EPS = 1e-6
LN_EPS = 1e-5
NEG_INF = -1e30
FFN_RESIDUAL = 0.5

LANES = 128
ROW_TILE = 1056
FFN_TILE = 256
PROJ_TILE = 512
KV_TILES_PER_SEQ = 3
KV_TILE = 256
OUT_TILE = 512
ATT_TILE = 384
BIAS_LANES = 256
PAGES_PER_STEP = 2
LOG2E = 1.4426950408889634
CONV_CHUNK = 264
CONV_LANES = 256
VMEM_LIMIT = 56 * 1024 * 1024


def _params(semantics):
    return pltpu.CompilerParams(dimension_semantics=semantics, vmem_limit_bytes=VMEM_LIMIT)


def _single(shape, index_map):
    return pl.BlockSpec(shape, index_map, pipeline_mode=pl.Buffered(1))


def _rms(x, g):
    ms = jnp.mean(x * x, axis=-1, keepdims=True)
    return x * lax.rsqrt(ms + RMS_EPS) * g


def _split3(x):
    hi = x.astype(BF16)
    r1 = x - hi.astype(F32)
    mid = r1.astype(BF16)
    lo = (r1 - mid.astype(F32)).astype(BF16)
    return hi, mid, lo


def _dot(a, b):
    return jnp.dot(a, b, preferred_element_type=F32)


def _dot_nt(a, b):
    return lax.dot_general(a, b, (((1,), (1,)), ((), ())), preferred_element_type=F32)


def _ffn_kernel(x_ref, g_ref, w1_ref, w3_ref, w2_ref, o_ref, xn_ref):
    @pl.when(pl.program_id(1) == 0)
    def _():
        x = x_ref[...]
        xn_ref[...] = _rms(x, g_ref[...]).astype(BF16)
        o_ref[...] = x

    xn = xn_ref[...]
    a = _dot(xn, w1_ref[...].astype(BF16))
    b = _dot(xn, w3_ref[...].astype(BF16))
    gate = (FFN_RESIDUAL * a * jax.nn.sigmoid(a) * b).astype(BF16)
    o_ref[...] += _dot(gate, w2_ref[...].astype(BF16))


def _ffn(h, g, w1, w3, w2):
    m, d = h.shape
    dff = w1.shape[1]
    return pl.pallas_call(
        _ffn_kernel,
        grid=(m // ROW_TILE, dff // FFN_TILE),
        in_specs=[
            _single((ROW_TILE, d), lambda i, f: (i, 0)),
            pl.BlockSpec((1, d), lambda i, f: (0, 0)),
            pl.BlockSpec((d, FFN_TILE), lambda i, f: (0, f)),
            pl.BlockSpec((d, FFN_TILE), lambda i, f: (0, f)),
            pl.BlockSpec((FFN_TILE, d), lambda i, f: (f, 0)),
        ],
        out_specs=pl.BlockSpec((ROW_TILE, d), lambda i, f: (i, 0)),
        out_shape=jax.ShapeDtypeStruct((m, d), F32),
        scratch_shapes=[pltpu.VMEM((ROW_TILE, d), BF16)],
        compiler_params=_params(("parallel", "arbitrary")),
        name="ffn",
    )(h, g.reshape(1, d), w1, w3, w2)


def _rms_proj_kernel(x_ref, g_ref, w_ref, o_ref, xn_ref, *, scale):
    @pl.when(pl.program_id(1) == 0)
    def _():
        xn_ref[...] = _rms(x_ref[...], g_ref[...]).astype(BF16)

    o_ref[...] = (_dot(xn_ref[...], w_ref[...].astype(BF16)) * scale).astype(o_ref.dtype)


def _rms_proj(h, g, w, scale, out_dtype):
    m, d = h.shape
    n = w.shape[1]
    return pl.pallas_call(
        functools.partial(_rms_proj_kernel, scale=scale),
        grid=(m // ROW_TILE, n // PROJ_TILE),
        in_specs=[
            _single((ROW_TILE, d), lambda i, j: (i, 0)),
            pl.BlockSpec((1, d), lambda i, j: (0, 0)),
            pl.BlockSpec((d, PROJ_TILE), lambda i, j: (0, j)),
        ],
        out_specs=pl.BlockSpec((ROW_TILE, PROJ_TILE), lambda i, j: (i, j)),
        out_shape=jax.ShapeDtypeStruct((m, n), out_dtype),
        scratch_shapes=[pltpu.VMEM((ROW_TILE, d), BF16)],
        compiler_params=_params(("parallel", "arbitrary")),
        name="rms_proj",
    )(h, g.reshape(1, d), w)


def _rms_glu_kernel(x_ref, g_ref, wa_ref, wg_ref, ba_ref, bg_ref, o_ref, xn_ref):
    @pl.when(pl.program_id(1) == 0)
    def _():
        xn_ref[...] = _rms(x_ref[...], g_ref[...]).astype(BF16)

    xn = xn_ref[...]
    a = _dot(xn, wa_ref[...].astype(BF16)) + ba_ref[...]
    gt = _dot(xn, wg_ref[...].astype(BF16)) + bg_ref[...]
    o_ref[...] = a * jax.nn.sigmoid(gt)


def _rms_glu(h, g, w_in, b_in):
    m, d = h.shape
    nt = d // PROJ_TILE
    return pl.pallas_call(
        _rms_glu_kernel,
        grid=(m // ROW_TILE, nt),
        in_specs=[
            _single((ROW_TILE, d), lambda i, j: (i, 0)),
            pl.BlockSpec((1, d), lambda i, j: (0, 0)),
            pl.BlockSpec((d, PROJ_TILE), lambda i, j: (0, j)),
            pl.BlockSpec((d, PROJ_TILE), lambda i, j: (0, j + nt)),
            pl.BlockSpec((1, PROJ_TILE), lambda i, j: (0, j)),
            pl.BlockSpec((1, PROJ_TILE), lambda i, j: (0, j + nt)),
        ],
        out_specs=pl.BlockSpec((ROW_TILE, PROJ_TILE), lambda i, j: (i, j)),
        out_shape=jax.ShapeDtypeStruct((m, d), F32),
        scratch_shapes=[pltpu.VMEM((ROW_TILE, d), BF16)],
        compiler_params=_params(("parallel", "arbitrary")),
        name="rms_glu",
    )(h, g.reshape(1, d), w_in, w_in, b_in.reshape(1, 2 * d), b_in.reshape(1, 2 * d))


def _log_sigmoid(z):
    return jnp.minimum(z, 0.0) - jnp.log1p(jnp.exp(-jnp.abs(z)))


def _kv_kernel(x_ref, g_ref, wk_ref, wv_ref, wft_ref, bf_ref,
               k32_ref, v32_ref, k16_ref, v16_ref, lft_ref, xn_ref):
    @pl.when(pl.program_id(1) == 0)
    def _():
        xn = _rms(x_ref[...], g_ref[...]).astype(BF16)
        xn_ref[...] = xn
        lft_ref[...] = _log_sigmoid(_dot_nt(wft_ref[...].astype(BF16), xn) + bf_ref[...])

    xn = xn_ref[...]
    k = _dot(xn, wk_ref[...].astype(BF16))
    v = _dot(xn, wv_ref[...].astype(BF16))
    k32_ref[0] = k
    v32_ref[0] = v
    k16_ref[...] = k.astype(BF16)
    v16_ref[...] = v.astype(BF16)


def _kv(h, g, w_k, w_v, w_f, b_f, nb, tp):
    m, d = h.shape
    n = w_k.shape[1]
    nh = w_f.shape[1]
    tps = KV_TILES_PER_SEQ
    rows = m // nb // tps
    out32 = pl.BlockSpec((1, rows, KV_TILE), lambda i, j: (i // tps, i % tps, j))
    out16 = pl.BlockSpec((rows, KV_TILE), lambda i, j: (i, j))
    wspec = pl.BlockSpec((d, KV_TILE), lambda i, j: (0, j))
    return pl.pallas_call(
        _kv_kernel,
        grid=(nb * tps, n // KV_TILE),
        in_specs=[
            _single((rows, d), lambda i, j: (i, 0)),
            pl.BlockSpec((1, d), lambda i, j: (0, 0)),
            wspec, wspec,
            pl.BlockSpec((nh, d), lambda i, j: (0, 0)),
            pl.BlockSpec((nh, 1), lambda i, j: (0, 0)),
        ],
        out_specs=[out32, out32, out16, out16, pl.BlockSpec((nh, rows), lambda i, j: (0, i))],
        out_shape=[
            jax.ShapeDtypeStruct((nb, tp, n), F32), jax.ShapeDtypeStruct((nb, tp, n), F32),
            jax.ShapeDtypeStruct((m, n), BF16), jax.ShapeDtypeStruct((m, n), BF16),
            jax.ShapeDtypeStruct((nh, m), F32),
        ],
        scratch_shapes=[pltpu.VMEM((rows, d), BF16)],
        compiler_params=_params(("parallel", "arbitrary")),
        name="kv_proj",
    )(h, g.reshape(1, d), w_k, w_v, w_f.T, b_f.reshape(nh, 1))


def _kv_rows_kernel(x_ref, g_ref, wk_ref, wv_ref, k_ref, v_ref):
    xn = _rms(x_ref[...], g_ref[...]).astype(BF16)
    k_ref[...] = _dot(xn, wk_ref[...].astype(BF16))
    v_ref[...] = _dot(xn, wv_ref[...].astype(BF16))


def _kv_rows(x, g, w_k, w_v):
    r, d = x.shape
    n = w_k.shape[1]
    wspec = pl.BlockSpec((d, PROJ_TILE), lambda j: (0, j))
    out = pl.BlockSpec((r, PROJ_TILE), lambda j: (0, j))
    return pl.pallas_call(
        _kv_rows_kernel,
        grid=(n // PROJ_TILE,),
        in_specs=[pl.BlockSpec((r, d), lambda j: (0, 0)), pl.BlockSpec((1, d), lambda j: (0, 0)), wspec, wspec],
        out_specs=[out, out],
        out_shape=[jax.ShapeDtypeStruct((r, n), F32), jax.ShapeDtypeStruct((r, n), F32)],
        compiler_params=_params(("parallel",)),
        name="kv_rows",
    )(x, g.reshape(1, d), w_k, w_v)


def _ln_proj_res_kernel(x_ref, lg_ref, lb_ref, w_ref, b_ref, r_ref, o_ref, xn_ref):
    @pl.when(pl.program_id(1) == 0)
    def _():
        x = x_ref[...]
        mu = jnp.mean(x, axis=-1, keepdims=True)
        xc = x - mu
        var = jnp.mean(xc * xc, axis=-1, keepdims=True)
        y = xc * lax.rsqrt(var + LN_EPS) * lg_ref[...] + lb_ref[...]
        xn_ref[...] = (y * jax.nn.sigmoid(y)).astype(BF16)

    o_ref[...] = r_ref[...] + _dot(xn_ref[...], w_ref[...].astype(BF16)) + b_ref[...]


def _ln_proj_res(x, ln_g, ln_b, w, b, res):
    m, d = x.shape
    n = w.shape[1]
    tile = pl.BlockSpec((ROW_TILE, PROJ_TILE), lambda i, j: (i, j))
    return pl.pallas_call(
        _ln_proj_res_kernel,
        grid=(m // ROW_TILE, n // PROJ_TILE),
        in_specs=[
            _single((ROW_TILE, d), lambda i, j: (i, 0)),
            pl.BlockSpec((1, d), lambda i, j: (0, 0)),
            pl.BlockSpec((1, d), lambda i, j: (0, 0)),
            pl.BlockSpec((d, PROJ_TILE), lambda i, j: (0, j)),
            pl.BlockSpec((1, PROJ_TILE), lambda i, j: (0, j)),
            tile,
        ],
        out_specs=tile,
        out_shape=jax.ShapeDtypeStruct((m, n), F32),
        scratch_shapes=[pltpu.VMEM((ROW_TILE, d), BF16)],
        compiler_params=_params(("parallel", "arbitrary")),
        name="ln_proj_res",
    )(x, ln_g.reshape(1, d), ln_b.reshape(1, d), w, b.reshape(1, n), res)


def _proj_res_kernel(x_ref, w_ref, r_ref, o_ref):
    o_ref[...] = r_ref[...] + _dot(x_ref[...], w_ref[...].astype(BF16))


def _proj_res(x, w, res):
    m, d = x.shape
    n = w.shape[1]
    tile = pl.BlockSpec((ROW_TILE, PROJ_TILE), lambda i, j: (i, j))
    return pl.pallas_call(
        _proj_res_kernel,
        grid=(m // ROW_TILE, n // PROJ_TILE),
        in_specs=[
            pl.BlockSpec((ROW_TILE, d), lambda i, j: (i, 0)),
            pl.BlockSpec((d, PROJ_TILE), lambda i, j: (0, j)),
            tile,
        ],
        out_specs=tile,
        out_shape=jax.ShapeDtypeStruct((m, n), F32),
        compiler_params=_params(("parallel", "arbitrary")),
        name="proj_res",
    )(x, w, res)


def _rms_out_kernel(x_ref, g_ref, o_ref):
    o_ref[0] = _rms(x_ref[...], g_ref[...])


def _rms_out(h, g, nb, seq, tpad, first):
    m, d = h.shape
    return pl.pallas_call(
        _rms_out_kernel,
        grid=(nb, seq // OUT_TILE),
        in_specs=[
            pl.BlockSpec((pl.Element(OUT_TILE), pl.Element(d)),
                         lambda b, i: ((b * (tpad // 8) + first // 8 + i * (OUT_TILE // 8)) * 8, 0)),
            pl.BlockSpec((1, d), lambda b, i: (0, 0)),
        ],
        out_specs=pl.BlockSpec((1, OUT_TILE, d), lambda b, i: (b, i, 0)),
        out_shape=jax.ShapeDtypeStruct((nb, seq, d), F32),
        compiler_params=_params(("parallel", "parallel")),
        name="rms_out",
    )(h, g.reshape(1, d))


def _rms_rows_kernel(x_ref, g_ref, o_ref):
    o_ref[...] = _rms(x_ref[...], g_ref[...])


def _rms_rows(x, g):
    r, d = x.shape
    blk = pl.BlockSpec((r, d), lambda i: (0, 0))
    return pl.pallas_call(
        _rms_rows_kernel,
        grid=(1,),
        in_specs=[blk, pl.BlockSpec((1, d), lambda i: (0, 0))],
        out_specs=blk,
        out_shape=jax.ShapeDtypeStruct((r, d), F32),
        compiler_params=_params(("arbitrary",)),
        name="rms_rows",
    )(x, g.reshape(1, d))


def _dwconv_kernel(x_ref, w_ref, b_ref, o_ref, buf_ref, *, width):
    t_len = x_ref.shape[1]
    lanes = x_ref.shape[2]
    lead = 33 - width
    buf_ref[pl.ds(0, 32), :] = jnp.zeros((32, lanes), F32)
    buf_ref[pl.ds(32, t_len), :] = x_ref[0]
    bias = b_ref[...]

    def chunk(c, carry):
        t0 = pl.multiple_of(c * CONV_CHUNK, 8)
        win = buf_ref[pl.ds(t0, CONV_CHUNK + 32), :]
        acc = jnp.broadcast_to(bias, (CONV_CHUNK, lanes))
        for r in range(8):
            taps = range(r, width, 8)
            z = win[r + lead:r + lead + CONV_CHUNK + 8 * (len(taps) - 1)]
            for a, j in enumerate(taps):
                acc = acc + w_ref[pl.ds(j, 1), :] * z[8 * a:8 * a + CONV_CHUNK]
        o_ref[0, pl.ds(t0, CONV_CHUNK), :] = acc
        return carry

    lax.fori_loop(0, t_len // CONV_CHUNK, chunk, 0)


def _dwconv(glu3, dw_w, dw_b):
    nb, t_len, d = glu3.shape
    width = dw_w.shape[0]
    blk = pl.BlockSpec((1, t_len, CONV_LANES), lambda b, c: (b, 0, c))
    return pl.pallas_call(
        functools.partial(_dwconv_kernel, width=width),
        grid=(nb, d // CONV_LANES),
        in_specs=[
            blk,
            pl.BlockSpec((width, CONV_LANES), lambda b, c: (0, c)),
            pl.BlockSpec((1, CONV_LANES), lambda b, c: (0, c)),
        ],
        out_specs=blk,
        out_shape=jax.ShapeDtypeStruct(glu3.shape, F32),
        scratch_shapes=[pltpu.VMEM((t_len + 32, CONV_LANES), F32)],
        compiler_params=_params(("parallel", "parallel")),
        name="dwconv",
    )(glu3, dw_w, dw_b.reshape(1, d))


def _dwconv_step_kernel(s_ref, x_ref, w_ref, b_ref, y_ref, ns_ref, buf_ref, *, width):
    n_state = width - 1
    n_new = x_ref.shape[1]
    d = x_ref.shape[2]
    buf_ref[pl.ds(0, n_state), :] = s_ref[0]
    buf_ref[pl.ds(n_state, n_new), :] = x_ref[0]
    acc = jnp.broadcast_to(b_ref[...], (n_new, d))
    for j in range(width):
        acc = acc + w_ref[pl.ds(j, 1), :] * buf_ref[pl.ds(j, n_new), :]
    y_ref[0] = acc
    ns_ref[0] = buf_ref[pl.ds(n_new, n_state), :]


def _dwconv_step(state, glu_s, dw_w, dw_b):
    nb, n_state, d = state.shape
    n_new = glu_s.shape[1]
    width = dw_w.shape[0]
    return pl.pallas_call(
        functools.partial(_dwconv_step_kernel, width=width),
        grid=(nb,),
        in_specs=[
            pl.BlockSpec((1, n_state, d), lambda b: (b, 0, 0)),
            pl.BlockSpec((1, n_new, d), lambda b: (b, 0, 0)),
            pl.BlockSpec((width, d), lambda b: (0, 0)),
            pl.BlockSpec((1, d), lambda b: (0, 0)),
        ],
        out_specs=[
            pl.BlockSpec((1, n_new, d), lambda b: (b, 0, 0)),
            pl.BlockSpec((1, n_state, d), lambda b: (b, 0, 0)),
        ],
        out_shape=[
            jax.ShapeDtypeStruct((nb, n_new, d), F32),
            jax.ShapeDtypeStruct((nb, n_state, d), F32),
        ],
        scratch_shapes=[pltpu.VMEM((n_state + n_new + 2, d), F32)],
        compiler_params=_params(("parallel",)),
        name="dwconv_step",
    )(state, glu_s, dw_w, dw_b.reshape(1, d))


def _neg_cumsum_kernel(lf_ref, o_ref, carry_ref):
    @pl.when(pl.program_id(1) == 0)
    def _():
        carry_ref[...] = jnp.zeros_like(carry_ref)

    row = lax.broadcasted_iota(jnp.int32, (LANES, LANES), 0)
    col = lax.broadcasted_iota(jnp.int32, (LANES, LANES), 1)
    upper = (row <= col).astype(BF16)
    hi, mid, lo = _split3(lf_ref[...])
    cs = (_dot(lo, upper) + _dot(mid, upper)) + _dot(hi, upper) + carry_ref[...]
    o_ref[...] = cs * -LOG2E
    carry_ref[...] = jnp.broadcast_to(cs[:, LANES - 1:LANES], cs.shape)


def _neg_cumsum(lft, nb):
    nh, m = lft.shape
    steps = m // nb // LANES
    blk = pl.BlockSpec((nh, LANES), lambda b, j: (0, b * steps + j))
    return pl.pallas_call(
        _neg_cumsum_kernel,
        grid=(nb, steps),
        in_specs=[blk],
        out_specs=blk,
        out_shape=jax.ShapeDtypeStruct((nh, m), F32),
        scratch_shapes=[pltpu.VMEM((nh, LANES), F32)],
        compiler_params=_params(("arbitrary", "arbitrary")),
        name="neg_cumsum",
    )(lft)


def _softmax_step(s, m_i, l_i, acc, v):
    m_new = jnp.maximum(m_i, jnp.max(s, axis=-1, keepdims=True))
    p = jnp.exp2(s - m_new)
    alpha = jnp.exp2(m_i - m_new)
    l_new = alpha * l_i + jnp.sum(p, axis=-1, keepdims=True)
    acc_new = alpha * acc + _dot(p.astype(BF16), v)
    return m_new, l_new, acc_new


def _attn_kernel(q_ref, k_ref, v_ref, nc_ref, o_ref, s_ref):
    qi = pl.program_id(2)
    q = q_ref[...]
    tq, dh = q.shape
    win = 2 * ATT_TILE

    def window(kj):
        return pl.ds(pl.multiple_of(kj * ATT_TILE, ATT_TILE), win)

    def scores(kj):
        bias = jnp.concatenate([nc_ref[kj], nc_ref[kj + 1]], axis=1)
        return _dot_nt(q, k_ref[window(kj), :]) + bias

    full = lax.shift_right_logical(qi, 1)
    last = jnp.maximum(qi - 1, 0)
    s_ref[...] = scores(jnp.where(full > 0, 0, last))

    def body(p, carry):
        s = s_ref[...]
        s_ref[...] = scores(jnp.where(p + 1 < full, 2 * (p + 1), last))
        return _softmax_step(s, *carry, v_ref[window(2 * p), :])

    init = (jnp.full((tq, 1), NEG_INF, F32), jnp.zeros((tq, 1), F32), jnp.zeros((tq, dh), F32))
    carry = lax.fori_loop(0, full, body, init)

    row = lax.broadcasted_iota(jnp.int32, (tq, win), 0) + qi * ATT_TILE
    col = lax.broadcasted_iota(jnp.int32, (tq, win), 1) + last * ATT_TILE
    s = jnp.where((col <= row) & (col >= 2 * full * ATT_TILE), s_ref[...], NEG_INF)
    _, l_i, acc = _softmax_step(s, *carry, v_ref[window(last), :])
    o_ref[...] = (acc / l_i).astype(o_ref.dtype)


def _attention(q, k, v, negc, nb, nh):
    m, hd = q.shape
    dh = hd // nh
    t = m // nb
    nt = t // ATT_TILE
    assert t % ATT_TILE == 0 and nt >= 2
    negc4 = negc.reshape(nh, nb, nt, 1, ATT_TILE)
    return pl.pallas_call(
        _attn_kernel,
        grid=(nb, nh, nt),
        in_specs=[
            pl.BlockSpec((ATT_TILE, dh), lambda b, h, i: (b * nt + i, h)),
            pl.BlockSpec((t, dh), lambda b, h, i: (b, h)),
            pl.BlockSpec((t, dh), lambda b, h, i: (b, h)),
            pl.BlockSpec((None, None, nt, 1, ATT_TILE), lambda b, h, i: (h, b, 0, 0, 0)),
        ],
        out_specs=pl.BlockSpec((ATT_TILE, dh), lambda b, h, i: (b * nt + i, h)),
        out_shape=jax.ShapeDtypeStruct((m, hd), BF16),
        scratch_shapes=[pltpu.VMEM((ATT_TILE, 2 * ATT_TILE), F32)],
        compiler_params=_params(("parallel", "parallel", "arbitrary")),
        name="fox_prompt",
    )(q, k, v, negc4)


def _sample_bias_kernel(pt_ref, lf_ref, lfn_ref, o_ref, g_ref, low_ref, *, n_pages, nh):
    b = pl.program_id(0)
    rpp = lfn_ref.shape[1]
    n_rows = g_ref.shape[0]

    @pl.when(b == 0)
    def _():
        r = lax.broadcasted_iota(jnp.int32, (n_rows, n_rows), 0)
        c = lax.broadcasted_iota(jnp.int32, (n_rows, n_rows), 1)
        low_ref[...] = (c < r).astype(BF16)
        g_ref[...] = jnp.zeros_like(g_ref)

    def gather(p, carry):
        src = pl.multiple_of(pt_ref[b * n_pages + p] * rpp, rpp)
        dst = pl.multiple_of(p * rpp, rpp)
        g_ref[pl.ds(dst, rpp), :] = lf_ref[pl.ds(src, rpp), :]
        return carry

    lax.fori_loop(0, n_pages, gather, 0)
    g_ref[pl.ds(n_pages * rpp, rpp), :] = lfn_ref[0]

    li = lax.broadcasted_iota(jnp.int32, (BIAS_LANES, BIAS_LANES), 0)
    lj = lax.broadcasted_iota(jnp.int32, (BIAS_LANES, BIAS_LANES), 1)
    same_head = (li & (nh - 1)) == (lj & (nh - 1))
    t_row = same_head.astype(BF16)
    t_pre = (same_head & (li <= lj)).astype(BF16)
    hi, mid, lo = _split3(g_ref[...])
    within = (_dot(lo, t_pre) + _dot(mid, t_pre)) + _dot(hi, t_pre)
    row_tot = (_dot(lo, t_row) + _dot(mid, t_row)) + _dot(hi, t_row)
    hi, mid, lo = _split3(row_tot)
    low = low_ref[...]
    before = (_dot(low, lo) + _dot(low, mid)) + _dot(low, hi)
    o_ref[0] = (within + before) * -LOG2E


def _sample_bias(page_table, lf_tab, lf_new, nh):
    ns, n_pages = page_table.shape
    rpp = lf_new.shape[1]
    n_rows = -(-((n_pages + 1) * rpp) // 16) * 16
    grid_spec = pltpu.PrefetchScalarGridSpec(
        num_scalar_prefetch=1,
        grid=(ns,),
        in_specs=[
            _single(lf_tab.shape, lambda b, pt: (0, 0)),
            pl.BlockSpec((1, rpp, BIAS_LANES), lambda b, pt: (b, 0, 0)),
        ],
        out_specs=pl.BlockSpec((1, n_rows, BIAS_LANES), lambda b, pt: (b, 0, 0)),
        scratch_shapes=[pltpu.VMEM((n_rows, BIAS_LANES), F32), pltpu.VMEM((n_rows, n_rows), BF16)],
    )
    return pl.pallas_call(
        functools.partial(_sample_bias_kernel, n_pages=n_pages, nh=nh),
        grid_spec=grid_spec,
        out_shape=jax.ShapeDtypeStruct((ns, n_rows, BIAS_LANES), F32),
        compiler_params=_params(("arbitrary",)),
        name="sample_bias",
    )(page_table.reshape(-1), lf_tab, lf_new)


def _paged_attn_kernel(pt_ref, q_ref, *refs, nh, n_new):
    pps = PAGES_PER_STEP
    k_refs, v_refs = refs[:pps], refs[pps:2 * pps]
    nc_ref, kn_ref, vn_ref, o_ref, qm_ref, mask_ref, m_ref, l_ref, acc_ref = refs[2 * pps:]
    p = pl.program_id(1)
    n_steps = pl.num_programs(1)
    rows, dh = qm_ref.shape
    cols = kn_ref.shape[0]
    rpp = cols // BIAS_LANES
    q_shift = n_new.bit_length() - 1
    h_shift = nh.bit_length() - 1

    @pl.when(p == 0)
    def _():
        q = q_ref[0].astype(F32)
        qm_ref[...] = jnp.concatenate([q[:, h * dh:(h + 1) * dh] for h in range(nh)], axis=0).astype(BF16)
        r = lax.broadcasted_iota(jnp.int32, (rows, cols), 0)
        c = lax.broadcasted_iota(jnp.int32, (rows, cols), 1)
        mask_ref[...] = jnp.where((r >> q_shift) == (c & (nh - 1)), 0.0, NEG_INF)
        m_ref[...] = jnp.full_like(m_ref, NEG_INF)
        l_ref[...] = jnp.zeros_like(l_ref)
        acc_ref[...] = jnp.zeros_like(acc_ref)

    def update(ks, vs, bias_row0, causal):
        qm = qm_ref[...]
        logits = [_dot_nt(qm, k) for k in ks]
        state = (m_ref[...], l_ref[...], acc_ref[...])
        for j, (s, v) in enumerate(zip(logits, vs)):
            bias = nc_ref[0, pl.ds(bias_row0 + j * rpp, rpp), :]
            s = jnp.concatenate([s[:, i * BIAS_LANES:(i + 1) * BIAS_LANES] + bias[i:i + 1, :]
                                 for i in range(rpp)], axis=1) + mask_ref[...]
            if causal:
                qpos = lax.broadcasted_iota(jnp.int32, s.shape, 0) & (n_new - 1)
                kpos = lax.broadcasted_iota(jnp.int32, s.shape, 1) >> h_shift
                s = jnp.where(kpos <= qpos, s, NEG_INF)
            state = _softmax_step(s, *state, v)
        m_ref[...], l_ref[...], acc_ref[...] = state

    update([r[...].astype(BF16) for r in k_refs], [r[...].astype(BF16) for r in v_refs],
           pl.multiple_of(p * (pps * rpp), rpp), False)

    @pl.when(p == n_steps - 1)
    def _():
        update([kn_ref[...]], [vn_ref[...]], n_steps * pps * rpp, True)
        o = acc_ref[...] / l_ref[...]
        for h in range(nh):
            o_ref[0, :, h * dh:(h + 1) * dh] = o[h * n_new:(h + 1) * n_new, :]


def _paged_attention(page_table, q_s, k_rows, v_rows, negc, k_new, v_new, nh):
    ns, n_new, hd = q_s.shape
    dh = hd // nh
    n_pages = page_table.shape[1]
    cols = k_new.shape[0] // ns
    rows = nh * n_new
    pps = PAGES_PER_STEP
    assert n_new & (n_new - 1) == 0 and nh & (nh - 1) == 0 and cols % BIAS_LANES == 0
    assert n_pages % pps == 0

    def cache(j):
        return pl.BlockSpec((cols, dh), lambda b, p, pt: (pt[b * n_pages + p * pps + j], 0))

    caches = [cache(j) for j in range(pps)]
    new = pl.BlockSpec((cols, dh), lambda b, p, pt: (b, 0))
    per_b = lambda shape: pl.BlockSpec(shape, lambda b, p, pt: (b, 0, 0))
    grid_spec = pltpu.PrefetchScalarGridSpec(
        num_scalar_prefetch=1,
        grid=(ns, n_pages // pps),
        in_specs=[per_b((1, n_new, hd))] + caches + caches + [per_b((1,) + negc.shape[1:]), new, new],
        out_specs=per_b((1, n_new, hd)),
        scratch_shapes=[
            pltpu.VMEM((rows, dh), BF16),
            pltpu.VMEM((rows, cols), F32),
            pltpu.VMEM((rows, 1), F32),
            pltpu.VMEM((rows, 1), F32),
            pltpu.VMEM((rows, dh), F32),
        ],
    )
    return pl.pallas_call(
        functools.partial(_paged_attn_kernel, nh=nh, n_new=n_new),
        grid_spec=grid_spec,
        out_shape=jax.ShapeDtypeStruct((ns, n_new, hd), F32),
        compiler_params=_params(("parallel", "arbitrary")),
        name="fox_sample",
    )(page_table.reshape(-1), q_s, *([k_rows] * pps), *([v_rows] * pps), negc, k_new, v_new)


def kernel(x_prompt, x_sample, cache_k, cache_v, cache_logf, state_conv, page_table, meta_tokens, norm_ffn1, ffn1_w1, ffn1_w3, ffn1_w2, norm_mix, conv_w_in, conv_b_in, conv_dw_w, conv_dw_b, conv_ln_g, conv_ln_b, conv_w_out, conv_b_out, norm_kv, w_k, w_v, w_f, b_f, w_q, w_o, norm_ffn2, ffn2_w1, ffn2_w3, ffn2_w2, norm_final):
    nb, seq, d = x_prompt.shape
    ns, s_len, _ = x_sample.shape
    n_meta = meta_tokens.shape[0]
    nh = w_f.shape[1]
    dh = w_k.shape[1] // nh
    n_pool, page = cache_k.shape[0], cache_k.shape[1]
    width = conv_dw_w.shape[1]
    depth = norm_ffn1.shape[0]
    n_a = conv_w_in.shape[0]

    tp = seq + n_meta
    tpad = -(-(tp + ns * s_len) // ATT_TILE) * ATT_TILE
    m = nb * tpad
    s0 = (nb - 1) * tpad + tp
    n_s = ns * s_len
    assert m % ROW_TILE == 0 and tpad % (KV_TILES_PER_SEQ * LANES) == 0 and tpad % CONV_CHUNK == 0
    assert width <= 33 and d % CONV_LANES == 0 and seq % OUT_TILE == 0 and n_meta % 8 == 0
    assert (page * nh) % BIAS_LANES == 0 and BIAS_LANES % nh == 0
    assert s0 % 16 == 0 and tpad - tp >= n_s and depth == 2 * n_a == 2

    meta = meta_tokens.astype(F32)
    pieces = []
    for b in range(nb):
        pieces += [meta, x_prompt[b]]
        if b < nb - 1:
            pieces.append(jnp.zeros((tpad - tp, d), F32))
    pieces += [x_sample.reshape(n_s, d), jnp.zeros((tpad - tp - n_s, d), F32)]
    h = jnp.concatenate(pieces, axis=0)

    def prompt_rows(x):
        return x.reshape(nb, tpad, x.shape[-1])[:, :tp]

    def sample_rows(x):
        return x[s0:s0 + n_s]

    h = _ffn(h, norm_ffn1[0], ffn1_w1[0], ffn1_w3[0], ffn1_w2[0])
    glu = _rms_glu(h, norm_mix[0], conv_w_in[0], conv_b_in[0])
    y = _dwconv(glu.reshape(nb, tpad, d), conv_dw_w[0], conv_dw_b[0]).reshape(m, d)
    y_s, conv_s = _dwconv_step(state_conv[0], sample_rows(glu).reshape(ns, s_len, d),
                               conv_dw_w[0], conv_dw_b[0])
    y = lax.dynamic_update_slice(y, y_s.reshape(n_s, d), (s0, 0))
    conv_prompt = prompt_rows(glu)[None, :, tp - (width - 1):]
    conv_sample = conv_s[None]
    h = _ln_proj_res(y, conv_ln_g[0], conv_ln_b[0], conv_w_out[0], conv_b_out[0], h)
    h = _ffn(h, norm_ffn2[0], ffn2_w1[0], ffn2_w3[0], ffn2_w2[0])

    k_prompt, v_prompt, k16, v16, lft = _kv(h, norm_kv, w_k, w_v, w_f, b_f, nb, tp)
    k_s, v_s = _kv_rows(sample_rows(h), norm_kv, w_k, w_v)
    h = _ffn(h, norm_ffn1[1], ffn1_w1[1], ffn1_w3[1], ffn1_w2[1])
    q = _rms_proj(h, norm_mix[1], w_q[0], dh ** -0.5 * LOG2E, BF16)
    negc = _neg_cumsum(lft, nb)
    o = _attention(q, k16, v16, negc, nb, nh)

    def new_page(x):
        x = x.reshape(ns, s_len, x.shape[-1])
        return jnp.pad(x, ((0, 0), (0, page - s_len), (0, 0)))

    lf_rows = lft.T
    rpp = page * nh // BIAS_LANES
    negc_s = _sample_bias(page_table, cache_logf.reshape(n_pool * rpp, BIAS_LANES),
                          new_page(sample_rows(lf_rows)).reshape(ns, rpp, BIAS_LANES), nh)
    o_s = _paged_attention(
        page_table, sample_rows(q).reshape(ns, s_len, nh * dh),
        cache_k.reshape(n_pool * page * nh, dh), cache_v.reshape(n_pool * page * nh, dh), negc_s,
        new_page(sample_rows(k16)).reshape(ns * page * nh, dh),
        new_page(sample_rows(v16)).reshape(ns * page * nh, dh), nh)
    o = lax.dynamic_update_slice(o, o_s.reshape(n_s, nh * dh).astype(BF16), (s0, 0))
    h = _proj_res(o, w_o[0], h)
    h = _ffn(h, norm_ffn2[1], ffn2_w1[1], ffn2_w3[1], ffn2_w2[1])

    y_prompt = _rms_out(h, norm_final, nb, seq, tpad, n_meta)
    y_sample = _rms_rows(sample_rows(h), norm_final).reshape(ns, s_len, d)
    k_prompt = k_prompt.reshape(nb, tp, nh, dh)
    v_prompt = v_prompt.reshape(nb, tp, nh, dh)
    logf_prompt = prompt_rows(lf_rows)
    k_sample = k_s.reshape(ns, s_len, nh, dh)
    v_sample = v_s.reshape(ns, s_len, nh, dh)
    logf_sample = sample_rows(lf_rows).reshape(ns, s_len, nh)
    return (y_prompt, y_sample, k_prompt, v_prompt, logf_prompt, conv_prompt,
            k_sample, v_sample, logf_sample, conv_sample)
```

```python
import functools

import jax
import jax.numpy as jnp
from jax import lax
from jax.experimental import pallas as pl
from jax.experimental.pallas import tpu as pltpu

F32 = jnp.float32
BF16 = jnp.bfloat16

RMS_EPS = 1e-6
LN_EPS = 1e-5
NEG_INF = -1e30
FFN_RESIDUAL = 0.5

LANES = 128
ROW_TILE = 1056
FFN_TILE = 256
PROJ_TILE = 512
KV_TILES_PER_SEQ = 3
KV_TILE = 256
OUT_TILE = 512
ATT_TILE = 384
BIAS_LANES = 256
PAGES_PER_STEP = 4
LOG2E = 1.4426950408889634
CONV_CHUNK = 264
CONV_LANES = 256
VMEM_LIMIT = 56 * 1024 * 1024


def _params(semantics):
    return pltpu.CompilerParams(dimension_semantics=semantics, vmem_limit_bytes=VMEM_LIMIT)


def _single(shape, index_map):
    return pl.BlockSpec(shape, index_map, pipeline_mode=pl.Buffered(1))


def _rms(x, g):
    ms = jnp.mean(x * x, axis=-1, keepdims=True)
    return x * lax.rsqrt(ms + RMS_EPS) * g


def _split3(x):
    hi = x.astype(BF16)
    r1 = x - hi.astype(F32)
    mid = r1.astype(BF16)
    lo = (r1 - mid.astype(F32)).astype(BF16)
    return hi, mid, lo


def _dot(a, b):
    return jnp.dot(a, b, preferred_element_type=F32)


def _dot_nt(a, b):
    return lax.dot_general(a, b, (((1,), (1,)), ((), ())), preferred_element_type=F32)


def _ffn_kernel(x_ref, g_ref, w1_ref, w3_ref, w2_ref, o_ref, xn_ref, gate_ref):
    f = pl.program_id(1)
    last = pl.num_programs(1) - 1

    def gate():
        xn = xn_ref[...]
        a = _dot(xn, w1_ref[...].astype(BF16))
        b = _dot(xn, w3_ref[...].astype(BF16))
        return (FFN_RESIDUAL * a * jax.nn.sigmoid(a) * b).astype(BF16)

    def down(g):
        return _dot(g, w2_ref[...].astype(BF16))

    @pl.when(f == 0)
    def _():
        x = x_ref[...]
        xn_ref[...] = _rms(x, g_ref[...]).astype(BF16)
        o_ref[...] = x
        gate_ref[...] = gate()

    @pl.when((f > 0) & (f < last))
    def _():
        prev = gate_ref[...]
        gate_ref[...] = gate()
        o_ref[...] += down(prev)

    @pl.when(f == last)
    def _():
        o_ref[...] += down(gate_ref[...])


def _ffn(h, g, w1, w3, w2, layer):
    m, d = h.shape
    dff = w1.shape[2]
    nf = dff // FFN_TILE
    return pl.pallas_call(
        _ffn_kernel,
        grid=(m // ROW_TILE, nf + 1),
        in_specs=[
            _single((ROW_TILE, d), lambda i, f: (i, 0)),
            pl.BlockSpec((None, 1, d), lambda i, f: (layer, 0, 0)),
            pl.BlockSpec((None, d, FFN_TILE), lambda i, f: (layer, 0, jnp.minimum(f, nf - 1))),
            pl.BlockSpec((None, d, FFN_TILE), lambda i, f: (layer, 0, jnp.minimum(f, nf - 1))),
            pl.BlockSpec((None, FFN_TILE, d), lambda i, f: (layer, jnp.maximum(f - 1, 0), 0)),
        ],
        out_specs=pl.BlockSpec((ROW_TILE, d), lambda i, f: (i, 0)),
        out_shape=jax.ShapeDtypeStruct((m, d), F32),
        scratch_shapes=[pltpu.VMEM((ROW_TILE, d), BF16), pltpu.VMEM((ROW_TILE, FFN_TILE), BF16)],
        compiler_params=_params(("parallel", "arbitrary")),
        name="ffn",
    )(h, g.reshape(g.shape[0], 1, d), w1, w3, w2)


def _rms_proj_kernel(x_ref, g_ref, w_ref, o_ref, xn_ref, *, scale):
    @pl.when(pl.program_id(1) == 0)
    def _():
        xn_ref[...] = _rms(x_ref[...], g_ref[...]).astype(BF16)

    o_ref[...] = (_dot(xn_ref[...], w_ref[...].astype(BF16)) * scale).astype(o_ref.dtype)


def _rms_proj(h, g, w, scale, out_dtype):
    m, d = h.shape
    n = w.shape[1]
    return pl.pallas_call(
        functools.partial(_rms_proj_kernel, scale=scale),
        grid=(m // ROW_TILE, n // PROJ_TILE),
        in_specs=[
            pl.BlockSpec((ROW_TILE, d), lambda i, j: (i, 0)),
            pl.BlockSpec((1, d), lambda i, j: (0, 0)),
            pl.BlockSpec((d, PROJ_TILE), lambda i, j: (0, j)),
        ],
        out_specs=pl.BlockSpec((ROW_TILE, PROJ_TILE), lambda i, j: (i, j)),
        out_shape=jax.ShapeDtypeStruct((m, n), out_dtype),
        scratch_shapes=[pltpu.VMEM((ROW_TILE, d), BF16)],
        compiler_params=_params(("parallel", "arbitrary")),
        name="rms_proj",
    )(h, g.reshape(1, d), w)


def _rms_glu_kernel(x_ref, g_ref, wa_ref, wg_ref, ba_ref, bg_ref, o_ref, xn_ref):
    @pl.when(pl.program_id(1) == 0)
    def _():
        xn_ref[...] = _rms(x_ref[...], g_ref[...]).astype(BF16)

    xn = xn_ref[...]
    a = _dot(xn, wa_ref[...].astype(BF16)) + ba_ref[...]
    gt = _dot(xn, wg_ref[...].astype(BF16)) + bg_ref[...]
    o_ref[...] = a * jax.nn.sigmoid(gt)


def _rms_glu(h, g, w_in, b_in):
    m, d = h.shape
    nt = d // PROJ_TILE
    return pl.pallas_call(
        _rms_glu_kernel,
        grid=(m // ROW_TILE, nt),
        in_specs=[
            pl.BlockSpec((ROW_TILE, d), lambda i, j: (i, 0)),
            pl.BlockSpec((1, d), lambda i, j: (0, 0)),
            pl.BlockSpec((d, PROJ_TILE), lambda i, j: (0, j)),
            pl.BlockSpec((d, PROJ_TILE), lambda i, j: (0, j + nt)),
            pl.BlockSpec((1, PROJ_TILE), lambda i, j: (0, j)),
            pl.BlockSpec((1, PROJ_TILE), lambda i, j: (0, j + nt)),
        ],
        out_specs=pl.BlockSpec((ROW_TILE, PROJ_TILE), lambda i, j: (i, j)),
        out_shape=jax.ShapeDtypeStruct((m, d), F32),
        scratch_shapes=[pltpu.VMEM((ROW_TILE, d), BF16)],
        compiler_params=_params(("parallel", "arbitrary")),
        name="rms_glu",
    )(h, g.reshape(1, d), w_in, w_in, b_in.reshape(1, 2 * d), b_in.reshape(1, 2 * d))


def _log_sigmoid(z):
    return jnp.minimum(z, 0.0) - jnp.log1p(jnp.exp(-jnp.abs(z)))


def _kv_kernel(x_ref, g_ref, wk_ref, wv_ref, wft_ref, bf_ref,
               k32_ref, v32_ref, k16_ref, v16_ref, lft_ref, xn_ref):
    @pl.when(pl.program_id(1) == 0)
    def _():
        xn = _rms(x_ref[...], g_ref[...]).astype(BF16)
        xn_ref[...] = xn
        lft_ref[...] = _log_sigmoid(_dot_nt(wft_ref[...].astype(BF16), xn) + bf_ref[...])

    xn = xn_ref[...]
    k = _dot(xn, wk_ref[...].astype(BF16))
    v = _dot(xn, wv_ref[...].astype(BF16))
    k32_ref[0] = k
    v32_ref[0] = v
    k16_ref[...] = k.astype(BF16)
    v16_ref[...] = v.astype(BF16)


def _kv(h, g, w_k, w_v, w_f, b_f, nb, tp):
    m, d = h.shape
    n = w_k.shape[1]
    nh = w_f.shape[1]
    tps = KV_TILES_PER_SEQ
    rows = m // nb // tps
    out32 = pl.BlockSpec((1, rows, KV_TILE), lambda i, j: (i // tps, i % tps, j))
    out16 = pl.BlockSpec((rows, KV_TILE), lambda i, j: (i, j))
    wspec = pl.BlockSpec((d, KV_TILE), lambda i, j: (0, j))
    return pl.pallas_call(
        _kv_kernel,
        grid=(nb * tps, n // KV_TILE),
        in_specs=[
            pl.BlockSpec((rows, d), lambda i, j: (i, 0)),
            pl.BlockSpec((1, d), lambda i, j: (0, 0)),
            wspec, wspec,
            pl.BlockSpec((nh, d), lambda i, j: (0, 0)),
            pl.BlockSpec((nh, 1), lambda i, j: (0, 0)),
        ],
        out_specs=[out32, out32, out16, out16, pl.BlockSpec((nh, rows), lambda i, j: (0, i))],
        out_shape=[
            jax.ShapeDtypeStruct((nb, tp, n), F32), jax.ShapeDtypeStruct((nb, tp, n), F32),
            jax.ShapeDtypeStruct((m, n), BF16), jax.ShapeDtypeStruct((m, n), BF16),
            jax.ShapeDtypeStruct((nh, m), F32),
        ],
        scratch_shapes=[pltpu.VMEM((rows, d), BF16)],
        compiler_params=_params(("parallel", "arbitrary")),
        name="kv_proj",
    )(h, g.reshape(1, d), w_k, w_v, w_f.T, b_f.reshape(nh, 1))


def _kv_rows_kernel(x_ref, g_ref, wk_ref, wv_ref, k_ref, v_ref):
    xn = _rms(x_ref[...], g_ref[...]).astype(BF16)
    k_ref[...] = _dot(xn, wk_ref[...].astype(BF16))
    v_ref[...] = _dot(xn, wv_ref[...].astype(BF16))


def _kv_rows(x, g, w_k, w_v):
    r, d = x.shape
    n = w_k.shape[1]
    wspec = pl.BlockSpec((d, PROJ_TILE), lambda j: (0, j))
    out = pl.BlockSpec((r, PROJ_TILE), lambda j: (0, j))
    return pl.pallas_call(
        _kv_rows_kernel,
        grid=(n // PROJ_TILE,),
        in_specs=[pl.BlockSpec((r, d), lambda j: (0, 0)), pl.BlockSpec((1, d), lambda j: (0, 0)), wspec, wspec],
        out_specs=[out, out],
        out_shape=[jax.ShapeDtypeStruct((r, n), F32), jax.ShapeDtypeStruct((r, n), F32)],
        compiler_params=_params(("parallel",)),
        name="kv_rows",
    )(x, g.reshape(1, d), w_k, w_v)


def _ln_proj_res_kernel(x_ref, lg_ref, lb_ref, w_ref, b_ref, r_ref, o_ref, xn_ref):
    @pl.when(pl.program_id(1) == 0)
    def _():
        x = x_ref[...]
        mu = jnp.mean(x, axis=-1, keepdims=True)
        xc = x - mu
        var = jnp.mean(xc * xc, axis=-1, keepdims=True)
        y = xc * lax.rsqrt(var + LN_EPS) * lg_ref[...] + lb_ref[...]
        xn_ref[...] = (y * jax.nn.sigmoid(y)).astype(BF16)

    o_ref[...] = r_ref[...] + _dot(xn_ref[...], w_ref[...].astype(BF16)) + b_ref[...]


def _ln_proj_res(x, ln_g, ln_b, w, b, res):
    m, d = x.shape
    n = w.shape[1]
    tile = pl.BlockSpec((ROW_TILE, PROJ_TILE), lambda i, j: (i, j))
    return pl.pallas_call(
        _ln_proj_res_kernel,
        grid=(m // ROW_TILE, n // PROJ_TILE),
        in_specs=[
            pl.BlockSpec((ROW_TILE, d), lambda i, j: (i, 0)),
            pl.BlockSpec((1, d), lambda i, j: (0, 0)),
            pl.BlockSpec((1, d), lambda i, j: (0, 0)),
            pl.BlockSpec((d, PROJ_TILE), lambda i, j: (0, j)),
            pl.BlockSpec((1, PROJ_TILE), lambda i, j: (0, j)),
            tile,
        ],
        out_specs=tile,
        out_shape=jax.ShapeDtypeStruct((m, n), F32),
        scratch_shapes=[pltpu.VMEM((ROW_TILE, d), BF16)],
        compiler_params=_params(("parallel", "arbitrary")),
        name="ln_proj_res",
    )(x, ln_g.reshape(1, d), ln_b.reshape(1, d), w, b.reshape(1, n), res)


def _proj_res_kernel(x_ref, w_ref, r_ref, o_ref):
    o_ref[...] = r_ref[...] + _dot(x_ref[...], w_ref[...].astype(BF16))


def _proj_res(x, w, res):
    m, d = x.shape
    n = w.shape[1]
    tile = pl.BlockSpec((ROW_TILE, PROJ_TILE), lambda i, j: (i, j))
    return pl.pallas_call(
        _proj_res_kernel,
        grid=(m // ROW_TILE, n // PROJ_TILE),
        in_specs=[
            pl.BlockSpec((ROW_TILE, d), lambda i, j: (i, 0)),
            pl.BlockSpec((d, PROJ_TILE), lambda i, j: (0, j)),
            tile,
        ],
        out_specs=tile,
        out_shape=jax.ShapeDtypeStruct((m, n), F32),
        compiler_params=_params(("parallel", "arbitrary")),
        name="proj_res",
    )(x, w, res)


def _rms_out_kernel(x_ref, g_ref, o_ref):
    o_ref[0] = _rms(x_ref[...], g_ref[...])


def _rms_out(h, g, nb, seq, tpad, first):
    m, d = h.shape
    return pl.pallas_call(
        _rms_out_kernel,
        grid=(nb, seq // OUT_TILE),
        in_specs=[
            pl.BlockSpec((pl.Element(OUT_TILE), pl.Element(d)),
                         lambda b, i: ((b * (tpad // 8) + first // 8 + i * (OUT_TILE // 8)) * 8, 0)),
            pl.BlockSpec((1, d), lambda b, i: (0, 0)),
        ],
        out_specs=pl.BlockSpec((1, OUT_TILE, d), lambda b, i: (b, i, 0)),
        out_shape=jax.ShapeDtypeStruct((nb, seq, d), F32),
        compiler_params=_params(("parallel", "parallel")),
        name="rms_out",
    )(h, g.reshape(1, d))


def _rms_rows_kernel(x_ref, g_ref, o_ref):
    o_ref[...] = _rms(x_ref[...], g_ref[...])


def _rms_rows(x, g):
    r, d = x.shape
    blk = pl.BlockSpec((r, d), lambda i: (0, 0))
    return pl.pallas_call(
        _rms_rows_kernel,
        grid=(1,),
        in_specs=[blk, pl.BlockSpec((1, d), lambda i: (0, 0))],
        out_specs=blk,
        out_shape=jax.ShapeDtypeStruct((r, d), F32),
        compiler_params=_params(("arbitrary",)),
        name="rms_rows",
    )(x, g.reshape(1, d))


def _dwconv_kernel(x_ref, w_ref, b_ref, o_ref, buf_ref, *, width):
    t_len = x_ref.shape[1]
    lanes = x_ref.shape[2]
    lead = 33 - width
    buf_ref[pl.ds(0, 32), :] = jnp.zeros((32, lanes), F32)
    buf_ref[pl.ds(32, t_len), :] = x_ref[0]
    bias = b_ref[...]

    def chunk(c, carry):
        t0 = pl.multiple_of(c * CONV_CHUNK, 8)
        win = buf_ref[pl.ds(t0, CONV_CHUNK + 32), :]
        acc = jnp.broadcast_to(bias, (CONV_CHUNK, lanes))
        rows = CONV_CHUNK + 32
        for r in range(8):
            shift = r + lead
            z = win if shift == 0 else pltpu.roll(win, rows - shift, axis=0)
            for a, j in enumerate(range(r, width, 8)):
                acc = acc + w_ref[pl.ds(j, 1), :] * z[8 * a:8 * a + CONV_CHUNK]
        o_ref[0, pl.ds(t0, CONV_CHUNK), :] = acc
        return carry

    lax.fori_loop(0, t_len // CONV_CHUNK, chunk, 0)


def _dwconv(glu3, dw_w, dw_b):
    nb, t_len, d = glu3.shape
    width = dw_w.shape[0]
    blk = pl.BlockSpec((1, t_len, CONV_LANES), lambda b, c: (b, 0, c))
    return pl.pallas_call(
        functools.partial(_dwconv_kernel, width=width),
        grid=(nb, d // CONV_LANES),
        in_specs=[
            blk,
            pl.BlockSpec((width, CONV_LANES), lambda b, c: (0, c)),
            pl.BlockSpec((1, CONV_LANES), lambda b, c: (0, c)),
        ],
        out_specs=blk,
        out_shape=jax.ShapeDtypeStruct(glu3.shape, F32),
        scratch_shapes=[pltpu.VMEM((t_len + 32, CONV_LANES), F32)],
        compiler_params=_params(("parallel", "parallel")),
        name="dwconv",
    )(glu3, dw_w, dw_b.reshape(1, d))


def _dwconv_step_kernel(s_ref, x_ref, w_ref, b_ref, y_ref, ns_ref, buf_ref, *, width):
    n_state = width - 1
    n_new = x_ref.shape[1]
    d = x_ref.shape[2]
    buf_ref[pl.ds(0, n_state), :] = s_ref[0]
    buf_ref[pl.ds(n_state, n_new), :] = x_ref[0]
    acc = jnp.broadcast_to(b_ref[...], (n_new, d))
    for j in range(width):
        acc = acc + w_ref[pl.ds(j, 1), :] * buf_ref[pl.ds(j, n_new), :]
    y_ref[0] = acc
    ns_ref[0] = buf_ref[pl.ds(n_new, n_state), :]


def _dwconv_step(state, glu_s, dw_w, dw_b):
    nb, n_state, d = state.shape
    n_new = glu_s.shape[1]
    width = dw_w.shape[0]
    return pl.pallas_call(
        functools.partial(_dwconv_step_kernel, width=width),
        grid=(nb,),
        in_specs=[
            pl.BlockSpec((1, n_state, d), lambda b: (b, 0, 0)),
            pl.BlockSpec((1, n_new, d), lambda b: (b, 0, 0)),
            pl.BlockSpec((width, d), lambda b: (0, 0)),
            pl.BlockSpec((1, d), lambda b: (0, 0)),
        ],
        out_specs=[
            pl.BlockSpec((1, n_new, d), lambda b: (b, 0, 0)),
            pl.BlockSpec((1, n_state, d), lambda b: (b, 0, 0)),
        ],
        out_shape=[
            jax.ShapeDtypeStruct((nb, n_new, d), F32),
            jax.ShapeDtypeStruct((nb, n_state, d), F32),
        ],
        scratch_shapes=[pltpu.VMEM((n_state + n_new + 2, d), F32)],
        compiler_params=_params(("parallel",)),
        name="dwconv_step",
    )(state, glu_s, dw_w, dw_b.reshape(1, d))


def _neg_cumsum_kernel(lf_ref, o_ref, carry_ref):
    @pl.when(pl.program_id(1) == 0)
    def _():
        carry_ref[...] = jnp.zeros_like(carry_ref)

    row = lax.broadcasted_iota(jnp.int32, (LANES, LANES), 0)
    col = lax.broadcasted_iota(jnp.int32, (LANES, LANES), 1)
    upper = (row <= col).astype(BF16)
    hi, mid, lo = _split3(lf_ref[...])
    cs = (_dot(lo, upper) + _dot(mid, upper)) + _dot(hi, upper) + carry_ref[...]
    o_ref[...] = cs * -LOG2E
    carry_ref[...] = jnp.broadcast_to(cs[:, LANES - 1:LANES], cs.shape)


def _neg_cumsum(lft, nb):
    nh, m = lft.shape
    steps = m // nb // LANES
    blk = pl.BlockSpec((nh, LANES), lambda b, j: (0, b * steps + j))
    return pl.pallas_call(
        _neg_cumsum_kernel,
        grid=(nb, steps),
        in_specs=[blk],
        out_specs=blk,
        out_shape=jax.ShapeDtypeStruct((nh, m), F32),
        scratch_shapes=[pltpu.VMEM((nh, LANES), F32)],
        compiler_params=_params(("arbitrary", "arbitrary")),
        name="neg_cumsum",
    )(lft)


def _softmax_step(s, m_i, l_i, acc, v):
    m_new = jnp.maximum(m_i, jnp.max(s, axis=-1, keepdims=True))
    p = jnp.exp2(s - m_new)
    alpha = jnp.exp2(m_i - m_new)
    l_new = alpha * l_i + jnp.sum(p, axis=-1, keepdims=True)
    acc_new = alpha * acc + _dot(p.astype(BF16), v)
    return m_new, l_new, acc_new


def _attn_kernel(q_ref, k_ref, v_ref, nc_ref, o_ref, s_ref):
    qi = pl.program_id(2)
    q = q_ref[...]
    tq, dh = q.shape
    win = 2 * ATT_TILE

    def window(kj):
        return pl.ds(pl.multiple_of(kj * ATT_TILE, ATT_TILE), win)

    def scores(kj):
        bias = jnp.concatenate([nc_ref[kj], nc_ref[kj + 1]], axis=1)
        return _dot_nt(q, k_ref[window(kj), :]) + bias

    full = lax.shift_right_logical(qi, 1)
    last = jnp.maximum(qi - 1, 0)
    s_ref[...] = scores(jnp.where(full > 0, 0, last))

    def body(p, carry):
        s = s_ref[...]
        s_ref[...] = scores(jnp.where(p + 1 < full, 2 * (p + 1), last))
        return _softmax_step(s, *carry, v_ref[window(2 * p), :])

    init = (jnp.full((tq, 1), NEG_INF, F32), jnp.zeros((tq, 1), F32), jnp.zeros((tq, dh), F32))
    carry = lax.fori_loop(0, full, body, init)

    row = lax.broadcasted_iota(jnp.int32, (tq, win), 0) + qi * ATT_TILE
    col = lax.broadcasted_iota(jnp.int32, (tq, win), 1) + last * ATT_TILE
    s = jnp.where((col <= row) & (col >= 2 * full * ATT_TILE), s_ref[...], NEG_INF)
    _, l_i, acc = _softmax_step(s, *carry, v_ref[window(last), :])
    o_ref[...] = (acc / l_i).astype(o_ref.dtype)


def _attention(q, k, v, negc, nb, nh):
    m, hd = q.shape
    dh = hd // nh
    t = m // nb
    nt = t // ATT_TILE
    assert t % ATT_TILE == 0 and nt >= 2
    negc4 = negc.reshape(nh, nb, nt, 1, ATT_TILE)
    return pl.pallas_call(
        _attn_kernel,
        grid=(nb, nh, nt),
        in_specs=[
            pl.BlockSpec((ATT_TILE, dh), lambda b, h, i: (b * nt + i, h)),
            pl.BlockSpec((t, dh), lambda b, h, i: (b, h)),
            pl.BlockSpec((t, dh), lambda b, h, i: (b, h)),
            pl.BlockSpec((None, None, nt, 1, ATT_TILE), lambda b, h, i: (h, b, 0, 0, 0)),
        ],
        out_specs=pl.BlockSpec((ATT_TILE, dh), lambda b, h, i: (b * nt + i, h)),
        out_shape=jax.ShapeDtypeStruct((m, hd), BF16),
        scratch_shapes=[pltpu.VMEM((ATT_TILE, 2 * ATT_TILE), F32)],
        compiler_params=_params(("parallel", "parallel", "arbitrary")),
        name="fox_prompt",
    )(q, k, v, negc4)


def _sample_bias_kernel(pt_ref, lf_ref, lfn_ref, o_ref, g_ref, low_ref, *, n_pages, nh):
    b = pl.program_id(0)
    rpp = lfn_ref.shape[1]
    n_rows = g_ref.shape[0]

    @pl.when(b == 0)
    def _():
        r = lax.broadcasted_iota(jnp.int32, (n_rows, n_rows), 0)
        c = lax.broadcasted_iota(jnp.int32, (n_rows, n_rows), 1)
        low_ref[...] = (c < r).astype(BF16)
        g_ref[...] = jnp.zeros_like(g_ref)

    def gather(p, carry):
        src = pl.multiple_of(pt_ref[b * n_pages + p] * rpp, rpp)
        dst = pl.multiple_of(p * rpp, rpp)
        g_ref[pl.ds(dst, rpp), :] = lf_ref[pl.ds(src, rpp), :]
        return carry

    lax.fori_loop(0, n_pages, gather, 0)
    g_ref[pl.ds(n_pages * rpp, rpp), :] = lfn_ref[0]

    li = lax.broadcasted_iota(jnp.int32, (BIAS_LANES, BIAS_LANES), 0)
    lj = lax.broadcasted_iota(jnp.int32, (BIAS_LANES, BIAS_LANES), 1)
    same_head = (li & (nh - 1)) == (lj & (nh - 1))
    t_row = same_head.astype(BF16)
    t_pre = (same_head & (li <= lj)).astype(BF16)
    hi, mid, lo = _split3(g_ref[...])
    within = (_dot(lo, t_pre) + _dot(mid, t_pre)) + _dot(hi, t_pre)
    row_tot = (_dot(lo, t_row) + _dot(mid, t_row)) + _dot(hi, t_row)
    hi, mid, lo = _split3(row_tot)
    low = low_ref[...]
    before = (_dot(low, lo) + _dot(low, mid)) + _dot(low, hi)
    o_ref[0] = (within + before) * -LOG2E


def _sample_bias(page_table, lf_tab, lf_new, nh):
    ns, n_pages = page_table.shape
    rpp = lf_new.shape[1]
    n_rows = -(-((n_pages + 1) * rpp) // 16) * 16
    grid_spec = pltpu.PrefetchScalarGridSpec(
        num_scalar_prefetch=1,
        grid=(ns,),
        in_specs=[
            _single(lf_tab.shape, lambda b, pt: (0, 0)),
            pl.BlockSpec((1, rpp, BIAS_LANES), lambda b, pt: (b, 0, 0)),
        ],
        out_specs=pl.BlockSpec((1, n_rows, BIAS_LANES), lambda b, pt: (b, 0, 0)),
        scratch_shapes=[pltpu.VMEM((n_rows, BIAS_LANES), F32), pltpu.VMEM((n_rows, n_rows), BF16)],
    )
    return pl.pallas_call(
        functools.partial(_sample_bias_kernel, n_pages=n_pages, nh=nh),
        grid_spec=grid_spec,
        out_shape=jax.ShapeDtypeStruct((ns, n_rows, BIAS_LANES), F32),
        compiler_params=_params(("arbitrary",)),
        name="sample_bias",
    )(page_table.reshape(-1), lf_tab, lf_new)


def _paged_attn_kernel(pt_ref, q_ref, *refs, nh, n_new):
    pps = PAGES_PER_STEP
    k_refs, v_refs = refs[:pps], refs[pps:2 * pps]
    nc_ref, kn_ref, vn_ref, o_ref, qm_ref, mask_ref, m_ref, l_ref, acc_ref = refs[2 * pps:]
    p = pl.program_id(1)
    n_steps = pl.num_programs(1)
    rows, dh = qm_ref.shape
    cols = kn_ref.shape[0]
    rpp = cols // BIAS_LANES
    q_shift = n_new.bit_length() - 1
    h_shift = nh.bit_length() - 1

    @pl.when(p == 0)
    def _():
        q = q_ref[0].astype(F32)
        qm_ref[...] = jnp.concatenate([q[:, h * dh:(h + 1) * dh] for h in range(nh)], axis=0).astype(BF16)
        r = lax.broadcasted_iota(jnp.int32, (rows, cols), 0)
        c = lax.broadcasted_iota(jnp.int32, (rows, cols), 1)
        mask_ref[...] = jnp.where((r >> q_shift) == (c & (nh - 1)), 0.0, NEG_INF)
        m_ref[...] = jnp.full_like(m_ref, NEG_INF)
        l_ref[...] = jnp.zeros_like(l_ref)
        acc_ref[...] = jnp.zeros_like(acc_ref)

    def update(ks, vs, bias_row0, causal):
        qm = qm_ref[...]
        logits = [_dot_nt(qm, k) for k in ks]
        state = (m_ref[...], l_ref[...], acc_ref[...])
        for j, (s, v) in enumerate(zip(logits, vs)):
            bias = nc_ref[0, pl.ds(bias_row0 + j * rpp, rpp), :]
            s = jnp.concatenate([s[:, i * BIAS_LANES:(i + 1) * BIAS_LANES] + bias[i:i + 1, :]
                                 for i in range(rpp)], axis=1) + mask_ref[...]
            if causal:
                qpos = lax.broadcasted_iota(jnp.int32, s.shape, 0) & (n_new - 1)
                kpos = lax.broadcasted_iota(jnp.int32, s.shape, 1) >> h_shift
                s = jnp.where(kpos <= qpos, s, NEG_INF)
            state = _softmax_step(s, *state, v)
        m_ref[...], l_ref[...], acc_ref[...] = state

    update([r[...].astype(BF16) for r in k_refs], [r[...].astype(BF16) for r in v_refs],
           pl.multiple_of(p * (pps * rpp), rpp), False)

    @pl.when(p == n_steps - 1)
    def _():
        update([kn_ref[...]], [vn_ref[...]], n_steps * pps * rpp, True)
        o = acc_ref[...] / l_ref[...]
        for h in range(nh):
            o_ref[0, :, h * dh:(h + 1) * dh] = o[h * n_new:(h + 1) * n_new, :]


def _paged_attention(page_table, q_s, k_rows, v_rows, negc, k_new, v_new, nh):
    ns, n_new, hd = q_s.shape
    dh = hd // nh
    n_pages = page_table.shape[1]
    cols = k_new.shape[0] // ns
    rows = nh * n_new
    pps = PAGES_PER_STEP
    assert n_new & (n_new - 1) == 0 and nh & (nh - 1) == 0 and cols % BIAS_LANES == 0
    assert n_pages % pps == 0

    def cache(j):
        return pl.BlockSpec((cols, dh), lambda b, p, pt: (pt[b * n_pages + p * pps + j], 0))

    caches = [cache(j) for j in range(pps)]
    new = pl.BlockSpec((cols, dh), lambda b, p, pt: (b, 0))
    per_b = lambda shape: pl.BlockSpec(shape, lambda b, p, pt: (b, 0, 0))
    grid_spec = pltpu.PrefetchScalarGridSpec(
        num_scalar_prefetch=1,
        grid=(ns, n_pages // pps),
        in_specs=[per_b((1, n_new, hd))] + caches + caches + [per_b((1,) + negc.shape[1:]), new, new],
        out_specs=per_b((1, n_new, hd)),
        scratch_shapes=[
            pltpu.VMEM((rows, dh), BF16),
            pltpu.VMEM((rows, cols), F32),
            pltpu.VMEM((rows, 1), F32),
            pltpu.VMEM((rows, 1), F32),
            pltpu.VMEM((rows, dh), F32),
        ],
    )
    return pl.pallas_call(
        functools.partial(_paged_attn_kernel, nh=nh, n_new=n_new),
        grid_spec=grid_spec,
        out_shape=jax.ShapeDtypeStruct((ns, n_new, hd), F32),
        compiler_params=_params(("parallel", "arbitrary")),
        name="fox_sample",
    )(page_table.reshape(-1), q_s, *([k_rows] * pps), *([v_rows] * pps), negc, k_new, v_new)


def kernel(x_prompt, x_sample, cache_k, cache_v, cache_logf, state_conv, page_table, meta_tokens, norm_ffn1, ffn1_w1, ffn1_w3, ffn1_w2, norm_mix, conv_w_in, conv_b_in, conv_dw_w, conv_dw_b, conv_ln_g, conv_ln_b, conv_w_out, conv_b_out, norm_kv, w_k, w_v, w_f, b_f, w_q, w_o, norm_ffn2, ffn2_w1, ffn2_w3, ffn2_w2, norm_final):
    nb, seq, d = x_prompt.shape
    ns, s_len, _ = x_sample.shape
    n_meta = meta_tokens.shape[0]
    nh = w_f.shape[1]
    dh = w_k.shape[1] // nh
    n_pool, page = cache_k.shape[0], cache_k.shape[1]
    width = conv_dw_w.shape[1]
    depth = norm_ffn1.shape[0]
    n_a = conv_w_in.shape[0]

    tp = seq + n_meta
    tpad = -(-(tp + ns * s_len) // ATT_TILE) * ATT_TILE
    m = nb * tpad
    s0 = (nb - 1) * tpad + tp
    n_s = ns * s_len
    assert m % ROW_TILE == 0 and tpad % (KV_TILES_PER_SEQ * LANES) == 0 and tpad % CONV_CHUNK == 0
    assert width <= 33 and d % CONV_LANES == 0 and seq % OUT_TILE == 0 and n_meta % 8 == 0
    assert (page * nh) % BIAS_LANES == 0 and BIAS_LANES % nh == 0
    assert s0 % 16 == 0 and tpad - tp >= n_s and depth == 2 * n_a == 2

    meta = meta_tokens.astype(F32)
    pieces = []
    for b in range(nb):
        pieces += [meta, x_prompt[b]]
        if b < nb - 1:
            pieces.append(jnp.zeros((tpad - tp, d), F32))
    pieces += [x_sample.reshape(n_s, d), jnp.zeros((tpad - tp - n_s, d), F32)]
    h = jnp.concatenate(pieces, axis=0)

    def prompt_rows(x):
        return x.reshape(nb, tpad, x.shape[-1])[:, :tp]

    def sample_rows(x):
        return x[s0:s0 + n_s]

    h = _ffn(h, norm_ffn1, ffn1_w1, ffn1_w3, ffn1_w2, 0)
    glu = _rms_glu(h, norm_mix[0], conv_w_in[0], conv_b_in[0])
    y = _dwconv(glu.reshape(nb, tpad, d), conv_dw_w[0], conv_dw_b[0]).reshape(m, d)
    y_s, conv_s = _dwconv_step(state_conv[0], sample_rows(glu).reshape(ns, s_len, d),
                               conv_dw_w[0], conv_dw_b[0])
    y = lax.dynamic_update_slice(y, y_s.reshape(n_s, d), (s0, 0))
    conv_prompt = prompt_rows(glu)[None, :, tp - (width - 1):]
    conv_sample = conv_s[None]
    h = _ln_proj_res(y, conv_ln_g[0], conv_ln_b[0], conv_w_out[0], conv_b_out[0], h)
    h = _ffn(h, norm_ffn2, ffn2_w1, ffn2_w3, ffn2_w2, 0)

    k_prompt, v_prompt, k16, v16, lft = _kv(h, norm_kv, w_k, w_v, w_f, b_f, nb, tp)
    k_s, v_s = _kv_rows(sample_rows(h), norm_kv, w_k, w_v)
    h = _ffn(h, norm_ffn1, ffn1_w1, ffn1_w3, ffn1_w2, 1)
    q = _rms_proj(h, norm_mix[1], w_q[0], dh ** -0.5 * LOG2E, BF16)
    negc = _neg_cumsum(lft, nb)
    o = _attention(q, k16, v16, negc, nb, nh)

    def new_page(x):
        x = x.reshape(ns, s_len, x.shape[-1])
        return jnp.pad(x, ((0, 0), (0, page - s_len), (0, 0)))

    lf_rows = lft.T
    rpp = page * nh // BIAS_LANES
    negc_s = _sample_bias(page_table, cache_logf.reshape(n_pool * rpp, BIAS_LANES),
                          new_page(sample_rows(lf_rows)).reshape(ns, rpp, BIAS_LANES), nh)
    o_s = _paged_attention(
        page_table, sample_rows(q).reshape(ns, s_len, nh * dh),
        cache_k.reshape(n_pool * page * nh, dh), cache_v.reshape(n_pool * page * nh, dh), negc_s,
        new_page(sample_rows(k16)).reshape(ns * page * nh, dh),
        new_page(sample_rows(v16)).reshape(ns * page * nh, dh), nh)
    o = lax.dynamic_update_slice(o, o_s.reshape(n_s, nh * dh).astype(BF16), (s0, 0))
    h = _proj_res(o, w_o[0], h)
    h = _ffn(h, norm_ffn2, ffn2_w1, ffn2_w3, ffn2_w2, 1)

    y_prompt = _rms_out(h, norm_final, nb, seq, tpad, n_meta)
    y_sample = _rms_rows(sample_rows(h), norm_final).reshape(ns, s_len, d)
    k_prompt = k_prompt.reshape(nb, tp, nh, dh)
    v_prompt = v_prompt.reshape(nb, tp, nh, dh)
    logf_prompt = prompt_rows(lf_rows)
    k_sample = k_s.reshape(ns, s_len, nh, dh)
    v_sample = v_s.reshape(ns, s_len, nh, dh)
    logf_sample = sample_rows(lf_rows).reshape(ns, s_len, nh)
    return (y_prompt, y_sample, k_prompt, v_prompt, logf_prompt, conv_prompt,
            k_sample, v_sample, logf_sample, conv_sample)
```

```python
import functools

import jax
import jax.numpy as jnp
from jax import lax
from jax.experimental import pallas as pl
from jax.experimental.pallas import tpu as pltpu

F32 = jnp.float32
BF16 = jnp.bfloat16

RMS_EPS = 1e-6
LN_EPS = 1e-5
NEG_INF = -1e30
FFN_RESIDUAL = 0.5

LANES = 128
ROW_TILE = 1056
FFN_ROWS = 2112
FFN_CHUNKS = 4
FFN_TILE = 256
PROJ_TILE = 512
KV_TILES_PER_SEQ = 3
KV_TILE = 256
OUT_TILE = 512
ATT_TILE = 384
BIAS_LANES = 256
PAGES_PER_STEP = 4
LOG2E = 1.4426950408889634
CONV_CHUNK = 264
CONV_LANES = 256
VMEM_LIMIT = 56 * 1024 * 1024


def _params(semantics):
    return pltpu.CompilerParams(dimension_semantics=semantics, vmem_limit_bytes=VMEM_LIMIT)


def _single(shape, index_map):
    return pl.BlockSpec(shape, index_map, pipeline_mode=pl.Buffered(1))


def _rms(x, g):
    ms = jnp.mean(x * x, axis=-1, keepdims=True)
    return x * lax.rsqrt(ms + RMS_EPS) * g


def _split3(x):
    hi = x.astype(BF16)
    r1 = x - hi.astype(F32)
    mid = r1.astype(BF16)
    lo = (r1 - mid.astype(F32)).astype(BF16)
    return hi, mid, lo


def _dot(a, b):
    return jnp.dot(a, b, preferred_element_type=F32)


def _dot_nt(a, b):
    return lax.dot_general(a, b, (((1,), (1,)), ((), ())), preferred_element_type=F32)


def _ffn_kernel(x_hbm, g_ref, w1_ref, w3_ref, w2_ref, o_hbm, acc_ref, xn_ref, gate_ref, sem):
    i = pl.program_id(0)
    f = pl.program_id(1)
    last = pl.num_programs(1) - 1
    tile = acc_ref.shape[0]
    chunk = tile // FFN_CHUNKS

    def rows(c):
        return pl.ds(c * chunk, chunk)

    def chunk_copy(c, store):
        hbm = (o_hbm if store else x_hbm).at[pl.ds(pl.multiple_of(i * tile + c * chunk, 8), chunk)]
        vmem = acc_ref.at[rows(c)]
        return pltpu.make_async_copy(vmem, hbm, sem.at[c]) if store else pltpu.make_async_copy(hbm, vmem, sem.at[c])

    def gate():
        xn = xn_ref[...]
        a = _dot(xn, w1_ref[...].astype(BF16))
        b = _dot(xn, w3_ref[...].astype(BF16))
        return (FFN_RESIDUAL * a * jax.nn.sigmoid(a) * b).astype(BF16)

    @pl.when(f == 0)
    def _():
        for c in range(FFN_CHUNKS):
            chunk_copy(c, False).start()
        for c in range(FFN_CHUNKS):
            chunk_copy(c, False).wait()
            xn_ref[rows(c), :] = _rms(acc_ref[rows(c), :], g_ref[...]).astype(BF16)
        gate_ref[...] = gate()

    @pl.when((f > 0) & (f < last))
    def _():
        prev = gate_ref[...]
        gate_ref[...] = gate()
        acc_ref[...] += _dot(prev, w2_ref[...].astype(BF16))

    @pl.when(f == last)
    def _():
        w2 = w2_ref[...].astype(BF16)
        for c in range(FFN_CHUNKS):
            acc_ref[rows(c), :] += _dot(gate_ref[rows(c), :], w2)
            chunk_copy(c, True).start()
        for c in range(FFN_CHUNKS):
            chunk_copy(c, True).wait()


def _ffn(h, g, w1, w3, w2, layer):
    m, d = h.shape
    dff = w1.shape[2]
    nf = dff // FFN_TILE
    assert m % FFN_ROWS == 0 and FFN_ROWS % (16 * FFN_CHUNKS) == 0
    return pl.pallas_call(
        _ffn_kernel,
        grid=(m // FFN_ROWS, nf + 1),
        in_specs=[
            pl.BlockSpec(memory_space=pl.ANY),
            pl.BlockSpec((None, 1, d), lambda i, f: (layer, 0, 0)),
            pl.BlockSpec((None, d, FFN_TILE), lambda i, f: (layer, 0, jnp.minimum(f, nf - 1))),
            pl.BlockSpec((None, d, FFN_TILE), lambda i, f: (layer, 0, jnp.minimum(f, nf - 1))),
            pl.BlockSpec((None, FFN_TILE, d), lambda i, f: (layer, jnp.maximum(f - 1, 0), 0)),
        ],
        out_specs=pl.BlockSpec(memory_space=pl.ANY),
        out_shape=jax.ShapeDtypeStruct((m, d), F32),
        scratch_shapes=[
            pltpu.VMEM((FFN_ROWS, d), F32),
            pltpu.VMEM((FFN_ROWS, d), BF16),
            pltpu.VMEM((FFN_ROWS, FFN_TILE), BF16),
            pltpu.SemaphoreType.DMA((FFN_CHUNKS,)),
        ],
        compiler_params=_params(("arbitrary", "arbitrary")),
        name="ffn",
    )(h, g.reshape(g.shape[0], 1, d), w1, w3, w2)


def _rms_proj_kernel(x_ref, g_ref, w_ref, o_ref, xn_ref, *, scale):
    @pl.when(pl.program_id(1) == 0)
    def _():
        xn_ref[...] = _rms(x_ref[...], g_ref[...]).astype(BF16)

    o_ref[...] = (_dot(xn_ref[...], w_ref[...].astype(BF16)) * scale).astype(o_ref.dtype)


def _rms_proj(h, g, w, scale, out_dtype):
    m, d = h.shape
    n = w.shape[1]
    return pl.pallas_call(
        functools.partial(_rms_proj_kernel, scale=scale),
        grid=(m // ROW_TILE, n // PROJ_TILE),
        in_specs=[
            pl.BlockSpec((ROW_TILE, d), lambda i, j: (i, 0)),
            pl.BlockSpec((1, d), lambda i, j: (0, 0)),
            pl.BlockSpec((d, PROJ_TILE), lambda i, j: (0, j)),
        ],
        out_specs=pl.BlockSpec((ROW_TILE, PROJ_TILE), lambda i, j: (i, j)),
        out_shape=jax.ShapeDtypeStruct((m, n), out_dtype),
        scratch_shapes=[pltpu.VMEM((ROW_TILE, d), BF16)],
        compiler_params=_params(("parallel", "arbitrary")),
        name="rms_proj",
    )(h, g.reshape(1, d), w)


def _rms_glu_kernel(x_ref, g_ref, wa_ref, wg_ref, ba_ref, bg_ref, o_ref, xn_ref):
    @pl.when(pl.program_id(1) == 0)
    def _():
        xn_ref[...] = _rms(x_ref[...], g_ref[...]).astype(BF16)

    xn = xn_ref[...]
    a = _dot(xn, wa_ref[...].astype(BF16)) + ba_ref[...]
    gt = _dot(xn, wg_ref[...].astype(BF16)) + bg_ref[...]
    o_ref[...] = a * jax.nn.sigmoid(gt)


def _rms_glu(h, g, w_in, b_in):
    m, d = h.shape
    nt = d // PROJ_TILE
    return pl.pallas_call(
        _rms_glu_kernel,
        grid=(m // ROW_TILE, nt),
        in_specs=[
            pl.BlockSpec((ROW_TILE, d), lambda i, j: (i, 0)),
            pl.BlockSpec((1, d), lambda i, j: (0, 0)),
            pl.BlockSpec((d, PROJ_TILE), lambda i, j: (0, j)),
            pl.BlockSpec((d, PROJ_TILE), lambda i, j: (0, j + nt)),
            pl.BlockSpec((1, PROJ_TILE), lambda i, j: (0, j)),
            pl.BlockSpec((1, PROJ_TILE), lambda i, j: (0, j + nt)),
        ],
        out_specs=pl.BlockSpec((ROW_TILE, PROJ_TILE), lambda i, j: (i, j)),
        out_shape=jax.ShapeDtypeStruct((m, d), F32),
        scratch_shapes=[pltpu.VMEM((ROW_TILE, d), BF16)],
        compiler_params=_params(("parallel", "arbitrary")),
        name="rms_glu",
    )(h, g.reshape(1, d), w_in, w_in, b_in.reshape(1, 2 * d), b_in.reshape(1, 2 * d))


def _log_sigmoid(z):
    return jnp.minimum(z, 0.0) - jnp.log1p(jnp.exp(-jnp.abs(z)))


def _kv_kernel(x_ref, g_ref, wk_ref, wv_ref, wft_ref, bf_ref,
               k32_ref, v32_ref, k16_ref, v16_ref, lft_ref, xn_ref):
    @pl.when(pl.program_id(1) == 0)
    def _():
        xn = _rms(x_ref[...], g_ref[...]).astype(BF16)
        xn_ref[...] = xn
        lft_ref[...] = _log_sigmoid(_dot_nt(wft_ref[...].astype(BF16), xn) + bf_ref[...])

    xn = xn_ref[...]
    k = _dot(xn, wk_ref[...].astype(BF16))
    v = _dot(xn, wv_ref[...].astype(BF16))
    k32_ref[0] = k
    v32_ref[0] = v
    k16_ref[...] = k.astype(BF16)
    v16_ref[...] = v.astype(BF16)


def _kv(h, g, w_k, w_v, w_f, b_f, nb, tp):
    m, d = h.shape
    n = w_k.shape[1]
    nh = w_f.shape[1]
    tps = KV_TILES_PER_SEQ
    rows = m // nb // tps
    out32 = pl.BlockSpec((1, rows, KV_TILE), lambda i, j: (i // tps, i % tps, j))
    out16 = pl.BlockSpec((rows, KV_TILE), lambda i, j: (i, j))
    wspec = pl.BlockSpec((d, KV_TILE), lambda i, j: (0, j))
    return pl.pallas_call(
        _kv_kernel,
        grid=(nb * tps, n // KV_TILE),
        in_specs=[
            pl.BlockSpec((rows, d), lambda i, j: (i, 0)),
            pl.BlockSpec((1, d), lambda i, j: (0, 0)),
            wspec, wspec,
            pl.BlockSpec((nh, d), lambda i, j: (0, 0)),
            pl.BlockSpec((nh, 1), lambda i, j: (0, 0)),
        ],
        out_specs=[out32, out32, out16, out16, pl.BlockSpec((nh, rows), lambda i, j: (0, i))],
        out_shape=[
            jax.ShapeDtypeStruct((nb, tp, n), F32), jax.ShapeDtypeStruct((nb, tp, n), F32),
            jax.ShapeDtypeStruct((m, n), BF16), jax.ShapeDtypeStruct((m, n), BF16),
            jax.ShapeDtypeStruct((nh, m), F32),
        ],
        scratch_shapes=[pltpu.VMEM((rows, d), BF16)],
        compiler_params=_params(("parallel", "arbitrary")),
        name="kv_proj",
    )(h, g.reshape(1, d), w_k, w_v, w_f.T, b_f.reshape(nh, 1))


def _kv_rows_kernel(x_ref, g_ref, wk_ref, wv_ref, k_ref, v_ref):
    xn = _rms(x_ref[...], g_ref[...]).astype(BF16)
    k_ref[...] = _dot(xn, wk_ref[...].astype(BF16))
    v_ref[...] = _dot(xn, wv_ref[...].astype(BF16))


def _kv_rows(x, g, w_k, w_v):
    r, d = x.shape
    n = w_k.shape[1]
    wspec = pl.BlockSpec((d, PROJ_TILE), lambda j: (0, j))
    out = pl.BlockSpec((r, PROJ_TILE), lambda j: (0, j))
    return pl.pallas_call(
        _kv_rows_kernel,
        grid=(n // PROJ_TILE,),
        in_specs=[pl.BlockSpec((r, d), lambda j: (0, 0)), pl.BlockSpec((1, d), lambda j: (0, 0)), wspec, wspec],
        out_specs=[out, out],
        out_shape=[jax.ShapeDtypeStruct((r, n), F32), jax.ShapeDtypeStruct((r, n), F32)],
        compiler_params=_params(("parallel",)),
        name="kv_rows",
    )(x, g.reshape(1, d), w_k, w_v)


def _ln_proj_res_kernel(x_ref, lg_ref, lb_ref, w_ref, b_ref, r_ref, o_ref, xn_ref):
    @pl.when(pl.program_id(1) == 0)
    def _():
        x = x_ref[...]
        mu = jnp.mean(x, axis=-1, keepdims=True)
        xc = x - mu
        var = jnp.mean(xc * xc, axis=-1, keepdims=True)
        y = xc * lax.rsqrt(var + LN_EPS) * lg_ref[...] + lb_ref[...]
        xn_ref[...] = (y * jax.nn.sigmoid(y)).astype(BF16)

    o_ref[...] = r_ref[...] + _dot(xn_ref[...], w_ref[...].astype(BF16)) + b_ref[...]


def _ln_proj_res(x, ln_g, ln_b, w, b, res):
    m, d = x.shape
    n = w.shape[1]
    tile = pl.BlockSpec((ROW_TILE, PROJ_TILE), lambda i, j: (i, j))
    return pl.pallas_call(
        _ln_proj_res_kernel,
        grid=(m // ROW_TILE, n // PROJ_TILE),
        in_specs=[
            pl.BlockSpec((ROW_TILE, d), lambda i, j: (i, 0)),
            pl.BlockSpec((1, d), lambda i, j: (0, 0)),
            pl.BlockSpec((1, d), lambda i, j: (0, 0)),
            pl.BlockSpec((d, PROJ_TILE), lambda i, j: (0, j)),
            pl.BlockSpec((1, PROJ_TILE), lambda i, j: (0, j)),
            tile,
        ],
        out_specs=tile,
        out_shape=jax.ShapeDtypeStruct((m, n), F32),
        scratch_shapes=[pltpu.VMEM((ROW_TILE, d), BF16)],
        compiler_params=_params(("parallel", "arbitrary")),
        name="ln_proj_res",
    )(x, ln_g.reshape(1, d), ln_b.reshape(1, d), w, b.reshape(1, n), res)


def _proj_res_kernel(x_ref, w_ref, r_ref, o_ref):
    o_ref[...] = r_ref[...] + _dot(x_ref[...], w_ref[...].astype(BF16))


def _proj_res(x, w, res):
    m, d = x.shape
    n = w.shape[1]
    tile = pl.BlockSpec((ROW_TILE, PROJ_TILE), lambda i, j: (i, j))
    return pl.pallas_call(
        _proj_res_kernel,
        grid=(m // ROW_TILE, n // PROJ_TILE),
        in_specs=[
            pl.BlockSpec((ROW_TILE, d), lambda i, j: (i, 0)),
            pl.BlockSpec((d, PROJ_TILE), lambda i, j: (0, j)),
            tile,
        ],
        out_specs=tile,
        out_shape=jax.ShapeDtypeStruct((m, n), F32),
        compiler_params=_params(("parallel", "arbitrary")),
        name="proj_res",
    )(x, w, res)


def _rms_out_kernel(x_ref, g_ref, o_ref):
    o_ref[0] = _rms(x_ref[...], g_ref[...])


def _rms_out(h, g, nb, seq, tpad, first):
    m, d = h.shape
    return pl.pallas_call(
        _rms_out_kernel,
        grid=(nb, seq // OUT_TILE),
        in_specs=[
            pl.BlockSpec((pl.Element(OUT_TILE), pl.Element(d)),
                         lambda b, i: ((b * (tpad // 8) + first // 8 + i * (OUT_TILE // 8)) * 8, 0)),
            pl.BlockSpec((1, d), lambda b, i: (0, 0)),
        ],
        out_specs=pl.BlockSpec((1, OUT_TILE, d), lambda b, i: (b, i, 0)),
        out_shape=jax.ShapeDtypeStruct((nb, seq, d), F32),
        compiler_params=_params(("parallel", "parallel")),
        name="rms_out",
    )(h, g.reshape(1, d))


def _rms_rows_kernel(x_ref, g_ref, o_ref):
    o_ref[...] = _rms(x_ref[...], g_ref[...])


def _rms_rows(x, g):
    r, d = x.shape
    blk = pl.BlockSpec((r, d), lambda i: (0, 0))
    return pl.pallas_call(
        _rms_rows_kernel,
        grid=(1,),
        in_specs=[blk, pl.BlockSpec((1, d), lambda i: (0, 0))],
        out_specs=blk,
        out_shape=jax.ShapeDtypeStruct((r, d), F32),
        compiler_params=_params(("arbitrary",)),
        name="rms_rows",
    )(x, g.reshape(1, d))


def _dwconv_kernel(x_ref, w_ref, b_ref, o_ref, buf_ref, *, width):
    t_len = x_ref.shape[1]
    lanes = x_ref.shape[2]
    lead = 33 - width
    buf_ref[pl.ds(0, 32), :] = jnp.zeros((32, lanes), F32)
    buf_ref[pl.ds(32, t_len), :] = x_ref[0]
    bias = b_ref[...]

    def chunk(c, carry):
        t0 = pl.multiple_of(c * CONV_CHUNK, 8)
        win = buf_ref[pl.ds(t0, CONV_CHUNK + 32), :]
        acc = jnp.broadcast_to(bias, (CONV_CHUNK, lanes))
        rows = CONV_CHUNK + 32
        for r in range(8):
            shift = r + lead
            z = win if shift == 0 else pltpu.roll(win, rows - shift, axis=0)
            for a, j in enumerate(range(r, width, 8)):
                acc = acc + w_ref[pl.ds(j, 1), :] * z[8 * a:8 * a + CONV_CHUNK]
        o_ref[0, pl.ds(t0, CONV_CHUNK), :] = acc
        return carry

    lax.fori_loop(0, t_len // CONV_CHUNK, chunk, 0)


def _dwconv(glu3, dw_w, dw_b):
    nb, t_len, d = glu3.shape
    width = dw_w.shape[0]
    blk = pl.BlockSpec((1, t_len, CONV_LANES), lambda b, c: (b, 0, c))
    return pl.pallas_call(
        functools.partial(_dwconv_kernel, width=width),
        grid=(nb, d // CONV_LANES),
        in_specs=[
            blk,
            pl.BlockSpec((width, CONV_LANES), lambda b, c: (0, c)),
            pl.BlockSpec((1, CONV_LANES), lambda b, c: (0, c)),
        ],
        out_specs=blk,
        out_shape=jax.ShapeDtypeStruct(glu3.shape, F32),
        scratch_shapes=[pltpu.VMEM((t_len + 32, CONV_LANES), F32)],
        compiler_params=_params(("parallel", "parallel")),
        name="dwconv",
    )(glu3, dw_w, dw_b.reshape(1, d))


def _dwconv_step_kernel(s_ref, x_ref, w_ref, b_ref, y_ref, ns_ref, buf_ref, *, width):
    n_state = width - 1
    n_new = x_ref.shape[1]
    d = x_ref.shape[2]
    buf_ref[pl.ds(0, n_state), :] = s_ref[0]
    buf_ref[pl.ds(n_state, n_new), :] = x_ref[0]
    acc = jnp.broadcast_to(b_ref[...], (n_new, d))
    for j in range(width):
        acc = acc + w_ref[pl.ds(j, 1), :] * buf_ref[pl.ds(j, n_new), :]
    y_ref[0] = acc
    ns_ref[0] = buf_ref[pl.ds(n_new, n_state), :]


def _dwconv_step(state, glu_s, dw_w, dw_b):
    nb, n_state, d = state.shape
    n_new = glu_s.shape[1]
    width = dw_w.shape[0]
    return pl.pallas_call(
        functools.partial(_dwconv_step_kernel, width=width),
        grid=(nb,),
        in_specs=[
            pl.BlockSpec((1, n_state, d), lambda b: (b, 0, 0)),
            pl.BlockSpec((1, n_new, d), lambda b: (b, 0, 0)),
            pl.BlockSpec((width, d), lambda b: (0, 0)),
            pl.BlockSpec((1, d), lambda b: (0, 0)),
        ],
        out_specs=[
            pl.BlockSpec((1, n_new, d), lambda b: (b, 0, 0)),
            pl.BlockSpec((1, n_state, d), lambda b: (b, 0, 0)),
        ],
        out_shape=[
            jax.ShapeDtypeStruct((nb, n_new, d), F32),
            jax.ShapeDtypeStruct((nb, n_state, d), F32),
        ],
        scratch_shapes=[pltpu.VMEM((n_state + n_new + 2, d), F32)],
        compiler_params=_params(("parallel",)),
        name="dwconv_step",
    )(state, glu_s, dw_w, dw_b.reshape(1, d))


def _neg_cumsum_kernel(lf_ref, o_ref, carry_ref):
    @pl.when(pl.program_id(1) == 0)
    def _():
        carry_ref[...] = jnp.zeros_like(carry_ref)

    row = lax.broadcasted_iota(jnp.int32, (LANES, LANES), 0)
    col = lax.broadcasted_iota(jnp.int32, (LANES, LANES), 1)
    upper = (row <= col).astype(BF16)
    hi, mid, lo = _split3(lf_ref[...])
    cs = (_dot(lo, upper) + _dot(mid, upper)) + _dot(hi, upper) + carry_ref[...]
    o_ref[...] = cs * -LOG2E
    carry_ref[...] = jnp.broadcast_to(cs[:, LANES - 1:LANES], cs.shape)


def _neg_cumsum(lft, nb):
    nh, m = lft.shape
    steps = m // nb // LANES
    blk = pl.BlockSpec((nh, LANES), lambda b, j: (0, b * steps + j))
    return pl.pallas_call(
        _neg_cumsum_kernel,
        grid=(nb, steps),
        in_specs=[blk],
        out_specs=blk,
        out_shape=jax.ShapeDtypeStruct((nh, m), F32),
        scratch_shapes=[pltpu.VMEM((nh, LANES), F32)],
        compiler_params=_params(("arbitrary", "arbitrary")),
        name="neg_cumsum",
    )(lft)


def _softmax_step(s, m_i, l_i, acc, v):
    m_new = jnp.maximum(m_i, jnp.max(s, axis=-1, keepdims=True))
    p = jnp.exp2(s - m_new)
    alpha = jnp.exp2(m_i - m_new)
    l_new = alpha * l_i + jnp.sum(p, axis=-1, keepdims=True)
    acc_new = alpha * acc + _dot(p.astype(BF16), v)
    return m_new, l_new, acc_new


def _attn_kernel(q_ref, k_ref, v_ref, nc_ref, o_ref, s_ref):
    qi = pl.program_id(2)
    q = q_ref[...]
    tq, dh = q.shape
    win = 2 * ATT_TILE

    def window(kj):
        return pl.ds(pl.multiple_of(kj * ATT_TILE, ATT_TILE), win)

    def scores(kj):
        bias = jnp.concatenate([nc_ref[kj], nc_ref[kj + 1]], axis=1)
        return _dot_nt(q, k_ref[window(kj), :]) + bias

    full = lax.shift_right_logical(qi, 1)
    last = jnp.maximum(qi - 1, 0)
    s_ref[...] = scores(jnp.where(full > 0, 0, last))

    def body(p, carry):
        s = s_ref[...]
        s_ref[...] = scores(jnp.where(p + 1 < full, 2 * (p + 1), last))
        return _softmax_step(s, *carry, v_ref[window(2 * p), :])

    init = (jnp.full((tq, 1), NEG_INF, F32), jnp.zeros((tq, 1), F32), jnp.zeros((tq, dh), F32))
    carry = lax.fori_loop(0, full, body, init)

    row = lax.broadcasted_iota(jnp.int32, (tq, win), 0) + qi * ATT_TILE
    col = lax.broadcasted_iota(jnp.int32, (tq, win), 1) + last * ATT_TILE
    s = jnp.where((col <= row) & (col >= 2 * full * ATT_TILE), s_ref[...], NEG_INF)
    _, l_i, acc = _softmax_step(s, *carry, v_ref[window(last), :])
    o_ref[...] = (acc / l_i).astype(o_ref.dtype)


def _attention(q, k, v, negc, nb, nh):
    m, hd = q.shape
    dh = hd // nh
    t = m // nb
    nt = t // ATT_TILE
    assert t % ATT_TILE == 0 and nt >= 2
    negc4 = negc.reshape(nh, nb, nt, 1, ATT_TILE)
    return pl.pallas_call(
        _attn_kernel,
        grid=(nb, nh, nt),
        in_specs=[
            pl.BlockSpec((ATT_TILE, dh), lambda b, h, i: (b * nt + i, h)),
            pl.BlockSpec((t, dh), lambda b, h, i: (b, h)),
            pl.BlockSpec((t, dh), lambda b, h, i: (b, h)),
            pl.BlockSpec((None, None, nt, 1, ATT_TILE), lambda b, h, i: (h, b, 0, 0, 0)),
        ],
        out_specs=pl.BlockSpec((ATT_TILE, dh), lambda b, h, i: (b * nt + i, h)),
        out_shape=jax.ShapeDtypeStruct((m, hd), BF16),
        scratch_shapes=[pltpu.VMEM((ATT_TILE, 2 * ATT_TILE), F32)],
        compiler_params=_params(("parallel", "parallel", "arbitrary")),
        name="fox_prompt",
    )(q, k, v, negc4)


def _sample_bias_kernel(pt_ref, lf_ref, lfn_ref, o_ref, g_ref, low_ref, *, n_pages, nh):
    b = pl.program_id(0)
    rpp = lfn_ref.shape[1]
    n_rows = g_ref.shape[0]

    @pl.when(b == 0)
    def _():
        r = lax.broadcasted_iota(jnp.int32, (n_rows, n_rows), 0)
        c = lax.broadcasted_iota(jnp.int32, (n_rows, n_rows), 1)
        low_ref[...] = (c < r).astype(BF16)
        g_ref[...] = jnp.zeros_like(g_ref)

    def gather(p, carry):
        src = pl.multiple_of(pt_ref[b * n_pages + p] * rpp, rpp)
        dst = pl.multiple_of(p * rpp, rpp)
        g_ref[pl.ds(dst, rpp), :] = lf_ref[pl.ds(src, rpp), :]
        return carry

    lax.fori_loop(0, n_pages, gather, 0)
    g_ref[pl.ds(n_pages * rpp, rpp), :] = lfn_ref[0]

    li = lax.broadcasted_iota(jnp.int32, (BIAS_LANES, BIAS_LANES), 0)
    lj = lax.broadcasted_iota(jnp.int32, (BIAS_LANES, BIAS_LANES), 1)
    same_head = (li & (nh - 1)) == (lj & (nh - 1))
    t_row = same_head.astype(BF16)
    t_pre = (same_head & (li <= lj)).astype(BF16)
    hi, mid, lo = _split3(g_ref[...])
    within = (_dot(lo, t_pre) + _dot(mid, t_pre)) + _dot(hi, t_pre)
    row_tot = (_dot(lo, t_row) + _dot(mid, t_row)) + _dot(hi, t_row)
    hi, mid, lo = _split3(row_tot)
    low = low_ref[...]
    before = (_dot(low, lo) + _dot(low, mid)) + _dot(low, hi)
    o_ref[0] = (within + before) * -LOG2E


def _sample_bias(page_table, lf_tab, lf_new, nh):
    ns, n_pages = page_table.shape
    rpp = lf_new.shape[1]
    n_rows = -(-((n_pages + 1) * rpp) // 16) * 16
    grid_spec = pltpu.PrefetchScalarGridSpec(
        num_scalar_prefetch=1,
        grid=(ns,),
        in_specs=[
            _single(lf_tab.shape, lambda b, pt: (0, 0)),
            pl.BlockSpec((1, rpp, BIAS_LANES), lambda b, pt: (b, 0, 0)),
        ],
        out_specs=pl.BlockSpec((1, n_rows, BIAS_LANES), lambda b, pt: (b, 0, 0)),
        scratch_shapes=[pltpu.VMEM((n_rows, BIAS_LANES), F32), pltpu.VMEM((n_rows, n_rows), BF16)],
    )
    return pl.pallas_call(
        functools.partial(_sample_bias_kernel, n_pages=n_pages, nh=nh),
        grid_spec=grid_spec,
        out_shape=jax.ShapeDtypeStruct((ns, n_rows, BIAS_LANES), F32),
        compiler_params=_params(("arbitrary",)),
        name="sample_bias",
    )(page_table.reshape(-1), lf_tab, lf_new)


def _paged_attn_kernel(pt_ref, q_ref, *refs, nh, n_new):
    pps = PAGES_PER_STEP
    k_refs, v_refs = refs[:pps], refs[pps:2 * pps]
    nc_ref, kn_ref, vn_ref, o_ref, qm_ref, mask_ref, m_ref, l_ref, acc_ref = refs[2 * pps:]
    p = pl.program_id(1)
    n_steps = pl.num_programs(1)
    rows, dh = qm_ref.shape
    cols = kn_ref.shape[0]
    rpp = cols // BIAS_LANES
    q_shift = n_new.bit_length() - 1
    h_shift = nh.bit_length() - 1

    @pl.when(p == 0)
    def _():
        q = q_ref[0].astype(F32)
        qm_ref[...] = jnp.concatenate([q[:, h * dh:(h + 1) * dh] for h in range(nh)], axis=0).astype(BF16)
        r = lax.broadcasted_iota(jnp.int32, (rows, cols), 0)
        c = lax.broadcasted_iota(jnp.int32, (rows, cols), 1)
        mask_ref[...] = jnp.where((r >> q_shift) == (c & (nh - 1)), 0.0, NEG_INF)
        m_ref[...] = jnp.full_like(m_ref, NEG_INF)
        l_ref[...] = jnp.zeros_like(l_ref)
        acc_ref[...] = jnp.zeros_like(acc_ref)

    def update(ks, vs, bias_row0, causal):
        qm = qm_ref[...]
        logits = [_dot_nt(qm, k) for k in ks]
        state = (m_ref[...], l_ref[...], acc_ref[...])
        for j, (s, v) in enumerate(zip(logits, vs)):
            bias = nc_ref[0, pl.ds(bias_row0 + j * rpp, rpp), :]
            s = jnp.concatenate([s[:, i * BIAS_LANES:(i + 1) * BIAS_LANES] + bias[i:i + 1, :]
                                 for i in range(rpp)], axis=1) + mask_ref[...]
            if causal:
                qpos = lax.broadcasted_iota(jnp.int32, s.shape, 0) & (n_new - 1)
                kpos = lax.broadcasted_iota(jnp.int32, s.shape, 1) >> h_shift
                s = jnp.where(kpos <= qpos, s, NEG_INF)
            state = _softmax_step(s, *state, v)
        m_ref[...], l_ref[...], acc_ref[...] = state

    update([r[...].astype(BF16) for r in k_refs], [r[...].astype(BF16) for r in v_refs],
           pl.multiple_of(p * (pps * rpp), rpp), False)

    @pl.when(p == n_steps - 1)
    def _():
        update([kn_ref[...]], [vn_ref[...]], n_steps * pps * rpp, True)
        o = acc_ref[...] / l_ref[...]
        for h in range(nh):
            o_ref[0, :, h * dh:(h + 1) * dh] = o[h * n_new:(h + 1) * n_new, :]


def _paged_attention(page_table, q_s, k_rows, v_rows, negc, k_new, v_new, nh):
    ns, n_new, hd = q_s.shape
    dh = hd // nh
    n_pages = page_table.shape[1]
    cols = k_new.shape[0] // ns
    rows = nh * n_new
    pps = PAGES_PER_STEP
    assert n_new & (n_new - 1) == 0 and nh & (nh - 1) == 0 and cols % BIAS_LANES == 0
    assert n_pages % pps == 0

    def cache(j):
        return pl.BlockSpec((cols, dh), lambda b, p, pt: (pt[b * n_pages + p * pps + j], 0))

    caches = [cache(j) for j in range(pps)]
    new = pl.BlockSpec((cols, dh), lambda b, p, pt: (b, 0))
    per_b = lambda shape: pl.BlockSpec(shape, lambda b, p, pt: (b, 0, 0))
    grid_spec = pltpu.PrefetchScalarGridSpec(
        num_scalar_prefetch=1,
        grid=(ns, n_pages // pps),
        in_specs=[per_b((1, n_new, hd))] + caches + caches + [per_b((1,) + negc.shape[1:]), new, new],
        out_specs=per_b((1, n_new, hd)),
        scratch_shapes=[
            pltpu.VMEM((rows, dh), BF16),
            pltpu.VMEM((rows, cols), F32),
            pltpu.VMEM((rows, 1), F32),
            pltpu.VMEM((rows, 1), F32),
            pltpu.VMEM((rows, dh), F32),
        ],
    )
    return pl.pallas_call(
        functools.partial(_paged_attn_kernel, nh=nh, n_new=n_new),
        grid_spec=grid_spec,
        out_shape=jax.ShapeDtypeStruct((ns, n_new, hd), F32),
        compiler_params=_params(("parallel", "arbitrary")),
        name="fox_sample",
    )(page_table.reshape(-1), q_s, *([k_rows] * pps), *([v_rows] * pps), negc, k_new, v_new)


def kernel(x_prompt, x_sample, cache_k, cache_v, cache_logf, state_conv, page_table, meta_tokens, norm_ffn1, ffn1_w1, ffn1_w3, ffn1_w2, norm_mix, conv_w_in, conv_b_in, conv_dw_w, conv_dw_b, conv_ln_g, conv_ln_b, conv_w_out, conv_b_out, norm_kv, w_k, w_v, w_f, b_f, w_q, w_o, norm_ffn2, ffn2_w1, ffn2_w3, ffn2_w2, norm_final):
    nb, seq, d = x_prompt.shape
    ns, s_len, _ = x_sample.shape
    n_meta = meta_tokens.shape[0]
    nh = w_f.shape[1]
    dh = w_k.shape[1] // nh
    n_pool, page = cache_k.shape[0], cache_k.shape[1]
    width = conv_dw_w.shape[1]
    depth = norm_ffn1.shape[0]
    n_a = conv_w_in.shape[0]

    tp = seq + n_meta
    tpad = -(-(tp + ns * s_len) // ATT_TILE) * ATT_TILE
    m = nb * tpad
    s0 = (nb - 1) * tpad + tp
    n_s = ns * s_len
    assert m % ROW_TILE == 0 and tpad % (KV_TILES_PER_SEQ * LANES) == 0 and tpad % CONV_CHUNK == 0
    assert width <= 33 and d % CONV_LANES == 0 and seq % OUT_TILE == 0 and n_meta % 8 == 0
    assert (page * nh) % BIAS_LANES == 0 and BIAS_LANES % nh == 0
    assert s0 % 16 == 0 and tpad - tp >= n_s and depth == 2 * n_a == 2

    meta = meta_tokens.astype(F32)
    pieces = []
    for b in range(nb):
        pieces += [meta, x_prompt[b]]
        if b < nb - 1:
            pieces.append(jnp.zeros((tpad - tp, d), F32))
    pieces += [x_sample.reshape(n_s, d), jnp.zeros((tpad - tp - n_s, d), F32)]
    h = jnp.concatenate(pieces, axis=0)

    def prompt_rows(x):
        return x.reshape(nb, tpad, x.shape[-1])[:, :tp]

    def sample_rows(x):
        return x[s0:s0 + n_s]

    h = _ffn(h, norm_ffn1, ffn1_w1, ffn1_w3, ffn1_w2, 0)
    glu = _rms_glu(h, norm_mix[0], conv_w_in[0], conv_b_in[0])
    y = _dwconv(glu.reshape(nb, tpad, d), conv_dw_w[0], conv_dw_b[0]).reshape(m, d)
    y_s, conv_s = _dwconv_step(state_conv[0], sample_rows(glu).reshape(ns, s_len, d),
                               conv_dw_w[0], conv_dw_b[0])
    y = lax.dynamic_update_slice(y, y_s.reshape(n_s, d), (s0, 0))
    conv_prompt = prompt_rows(glu)[None, :, tp - (width - 1):]
    conv_sample = conv_s[None]
    h = _ln_proj_res(y, conv_ln_g[0], conv_ln_b[0], conv_w_out[0], conv_b_out[0], h)
    h = _ffn(h, norm_ffn2, ffn2_w1, ffn2_w3, ffn2_w2, 0)

    k_prompt, v_prompt, k16, v16, lft = _kv(h, norm_kv, w_k, w_v, w_f, b_f, nb, tp)
    k_s, v_s = _kv_rows(sample_rows(h), norm_kv, w_k, w_v)
    h = _ffn(h, norm_ffn1, ffn1_w1, ffn1_w3, ffn1_w2, 1)
    q = _rms_proj(h, norm_mix[1], w_q[0], dh ** -0.5 * LOG2E, BF16)
    negc = _neg_cumsum(lft, nb)
    o = _attention(q, k16, v16, negc, nb, nh)

    def new_page(x):
        x = x.reshape(ns, s_len, x.shape[-1])
        return jnp.pad(x, ((0, 0), (0, page - s_len), (0, 0)))

    lf_rows = lft.T
    rpp = page * nh // BIAS_LANES
    negc_s = _sample_bias(page_table, cache_logf.reshape(n_pool * rpp, BIAS_LANES),
                          new_page(sample_rows(lf_rows)).reshape(ns, rpp, BIAS_LANES), nh)
    o_s = _paged_attention(
        page_table, sample_rows(q).reshape(ns, s_len, nh * dh),
        cache_k.reshape(n_pool * page * nh, dh), cache_v.reshape(n_pool * page * nh, dh), negc_s,
        new_page(sample_rows(k16)).reshape(ns * page * nh, dh),
        new_page(sample_rows(v16)).reshape(ns * page * nh, dh), nh)
    o = lax.dynamic_update_slice(o, o_s.reshape(n_s, nh * dh).astype(BF16), (s0, 0))
    h = _proj_res(o, w_o[0], h)
    h = _ffn(h, norm_ffn2, ffn2_w1, ffn2_w3, ffn2_w2, 1)

    y_prompt = _rms_out(h, norm_final, nb, seq, tpad, n_meta)
    y_sample = _rms_rows(sample_rows(h), norm_final).reshape(ns, s_len, d)
    k_prompt = k_prompt.reshape(nb, tp, nh, dh)
    v_prompt = v_prompt.reshape(nb, tp, nh, dh)
    logf_prompt = prompt_rows(lf_rows)
    k_sample = k_s.reshape(ns, s_len, nh, dh)
    v_sample = v_s.reshape(ns, s_len, nh, dh)
    logf_sample = sample_rows(lf_rows).reshape(ns, s_len, nh)
    return (y_prompt, y_sample, k_prompt, v_prompt, logf_prompt, conv_prompt,
            k_sample, v_sample, logf_sample, conv_sample)
```

```python
import functools

import jax
import jax.numpy as jnp
from jax import lax
from jax.experimental import pallas as pl
from jax.experimental.pallas import tpu as pltpu

F32 = jnp.float32
BF16 = jnp.bfloat16

RMS_EPS = 1e-6
LN_EPS = 1e-5
NEG_INF = -1e30
FFN_RESIDUAL = 0.5

LANES = 128
ROW_TILE = 1056
FFN_ROWS = 2112
FFN_CHUNKS = 4
FFN_TILE = 256
PROJ_TILE = 512
SQ_ROW_TILE = 528
KV_TILES_PER_SEQ = 3
KV_TILE = 256
OUT_TILE = 512
ATT_TILE = 384
ATT_HEADS = 2
BIAS_LANES = 256
PAGES_PER_STEP = 4
LOG2E = 1.4426950408889634
CONV_CHUNK = 264
CONV_LANES = 256
VMEM_LIMIT = 56 * 1024 * 1024


def _params(semantics):
    return pltpu.CompilerParams(dimension_semantics=semantics, vmem_limit_bytes=VMEM_LIMIT)


def _single(shape, index_map):
    return pl.BlockSpec(shape, index_map, pipeline_mode=pl.Buffered(1))


def _rms(x, g):
    ms = jnp.mean(x * x, axis=-1, keepdims=True)
    return x * lax.rsqrt(ms + RMS_EPS) * g


def _split3(x):
    hi = x.astype(BF16)
    r1 = x - hi.astype(F32)
    mid = r1.astype(BF16)
    lo = (r1 - mid.astype(F32)).astype(BF16)
    return hi, mid, lo


def _dot(a, b):
    return jnp.dot(a, b, preferred_element_type=F32)


def _dot_nt(a, b):
    return lax.dot_general(a, b, (((1,), (1,)), ((), ())), preferred_element_type=F32)


def _ffn_kernel(x_hbm, g_ref, w1_ref, w3_ref, w2_ref, o_hbm, acc_ref, xn_ref, gate_ref, sem):
    i = pl.program_id(0)
    f = pl.program_id(1)
    last = pl.num_programs(1) - 1
    tile = acc_ref.shape[0]
    chunk = tile // FFN_CHUNKS

    def rows(c):
        return pl.ds(c * chunk, chunk)

    def chunk_copy(c, store):
        hbm = (o_hbm if store else x_hbm).at[pl.ds(pl.multiple_of(i * tile + c * chunk, 8), chunk)]
        vmem = acc_ref.at[rows(c)]
        return pltpu.make_async_copy(vmem, hbm, sem.at[c]) if store else pltpu.make_async_copy(hbm, vmem, sem.at[c])

    def gate():
        xn = xn_ref[...]
        a = _dot(xn, w1_ref[...].astype(BF16))
        b = _dot(xn, w3_ref[...].astype(BF16))
        return (FFN_RESIDUAL * a * jax.nn.sigmoid(a) * b).astype(BF16)

    @pl.when(f == 0)
    def _():
        for c in range(FFN_CHUNKS):
            chunk_copy(c, False).start()
        for c in range(FFN_CHUNKS):
            chunk_copy(c, False).wait()
            xn_ref[rows(c), :] = _rms(acc_ref[rows(c), :], g_ref[...]).astype(BF16)
        gate_ref[...] = gate()

    @pl.when((f > 0) & (f < last))
    def _():
        prev = gate_ref[...]
        gate_ref[...] = gate()
        acc_ref[...] += _dot(prev, w2_ref[...].astype(BF16))

    @pl.when(f == last)
    def _():
        w2 = w2_ref[...].astype(BF16)
        for c in range(FFN_CHUNKS):
            acc_ref[rows(c), :] += _dot(gate_ref[rows(c), :], w2)
            chunk_copy(c, True).start()
        for c in range(FFN_CHUNKS):
            chunk_copy(c, True).wait()


def _ffn(h, g, w1, w3, w2, layer):
    m, d = h.shape
    dff = w1.shape[2]
    nf = dff // FFN_TILE
    assert m % FFN_ROWS == 0 and FFN_ROWS % (16 * FFN_CHUNKS) == 0
    return pl.pallas_call(
        _ffn_kernel,
        grid=(m // FFN_ROWS, nf + 1),
        in_specs=[
            pl.BlockSpec(memory_space=pl.ANY),
            pl.BlockSpec((None, 1, d), lambda i, f: (layer, 0, 0)),
            pl.BlockSpec((None, d, FFN_TILE), lambda i, f: (layer, 0, jnp.minimum(f, nf - 1))),
            pl.BlockSpec((None, d, FFN_TILE), lambda i, f: (layer, 0, jnp.minimum(f, nf - 1))),
            pl.BlockSpec((None, FFN_TILE, d), lambda i, f: (layer, jnp.maximum(f - 1, 0), 0)),
        ],
        out_specs=pl.BlockSpec(memory_space=pl.ANY),
        out_shape=jax.ShapeDtypeStruct((m, d), F32),
        scratch_shapes=[
            pltpu.VMEM((FFN_ROWS, d), F32),
            pltpu.VMEM((FFN_ROWS, d), BF16),
            pltpu.VMEM((FFN_ROWS, FFN_TILE), BF16),
            pltpu.SemaphoreType.DMA((FFN_CHUNKS,)),
        ],
        compiler_params=_params(("arbitrary", "arbitrary")),
        name="ffn",
    )(h, g.reshape(g.shape[0], 1, d), w1, w3, w2)


def _cast_weight_once(w_ref, wb_ref):
    @pl.when(pl.program_id(0) == 0)
    def _():
        wb_ref[...] = w_ref[...].astype(BF16)


def _resident_weight_specs(d, n):
    return _single((d, n), lambda i: (0, 0)), pltpu.VMEM((d, n), BF16)


def _rms_proj_kernel(x_ref, g_ref, w_ref, o_ref, wb_ref, *, scale):
    _cast_weight_once(w_ref, wb_ref)
    xn = _rms(x_ref[...], g_ref[...]).astype(BF16)
    o_ref[...] = (_dot(xn, wb_ref[...]) * scale).astype(o_ref.dtype)


def _rms_proj(h, g, w, scale, out_dtype):
    m, d = h.shape
    n = w.shape[1]
    w_spec, w_scratch = _resident_weight_specs(d, n)
    return pl.pallas_call(
        functools.partial(_rms_proj_kernel, scale=scale),
        grid=(m // SQ_ROW_TILE,),
        in_specs=[
            pl.BlockSpec((SQ_ROW_TILE, d), lambda i: (i, 0)),
            pl.BlockSpec((1, d), lambda i: (0, 0)),
            w_spec,
        ],
        out_specs=pl.BlockSpec((SQ_ROW_TILE, n), lambda i: (i, 0)),
        out_shape=jax.ShapeDtypeStruct((m, n), out_dtype),
        scratch_shapes=[w_scratch],
        compiler_params=_params(("arbitrary",)),
        name="rms_proj",
    )(h, g.reshape(1, d), w)


def _rms_glu_kernel(x_ref, g_ref, wa_ref, wg_ref, ba_ref, bg_ref, o_ref, xn_ref):
    @pl.when(pl.program_id(1) == 0)
    def _():
        xn_ref[...] = _rms(x_ref[...], g_ref[...]).astype(BF16)

    xn = xn_ref[...]
    a = _dot(xn, wa_ref[...].astype(BF16)) + ba_ref[...]
    gt = _dot(xn, wg_ref[...].astype(BF16)) + bg_ref[...]
    o_ref[...] = a * jax.nn.sigmoid(gt)


def _rms_glu(h, g, w_in, b_in):
    m, d = h.shape
    nt = d // PROJ_TILE
    return pl.pallas_call(
        _rms_glu_kernel,
        grid=(m // ROW_TILE, nt),
        in_specs=[
            pl.BlockSpec((ROW_TILE, d), lambda i, j: (i, 0)),
            pl.BlockSpec((1, d), lambda i, j: (0, 0)),
            pl.BlockSpec((d, PROJ_TILE), lambda i, j: (0, j)),
            pl.BlockSpec((d, PROJ_TILE), lambda i, j: (0, j + nt)),
            pl.BlockSpec((1, PROJ_TILE), lambda i, j: (0, j)),
            pl.BlockSpec((1, PROJ_TILE), lambda i, j: (0, j + nt)),
        ],
        out_specs=pl.BlockSpec((ROW_TILE, PROJ_TILE), lambda i, j: (i, j)),
        out_shape=jax.ShapeDtypeStruct((m, d), F32),
        scratch_shapes=[pltpu.VMEM((ROW_TILE, d), BF16)],
        compiler_params=_params(("parallel", "arbitrary")),
        name="rms_glu",
    )(h, g.reshape(1, d), w_in, w_in, b_in.reshape(1, 2 * d), b_in.reshape(1, 2 * d))


def _log_sigmoid(z):
    return jnp.minimum(z, 0.0) - jnp.log1p(jnp.exp(-jnp.abs(z)))


def _kv_kernel(x_ref, g_ref, wk_ref, wv_ref, wft_ref, bf_ref,
               k32_ref, v32_ref, k16_ref, v16_ref, lft_ref, xn_ref):
    @pl.when(pl.program_id(1) == 0)
    def _():
        xn = _rms(x_ref[...], g_ref[...]).astype(BF16)
        xn_ref[...] = xn
        lft_ref[...] = _log_sigmoid(_dot_nt(wft_ref[...].astype(BF16), xn) + bf_ref[...])

    xn = xn_ref[...]
    k = _dot(xn, wk_ref[...].astype(BF16))
    v = _dot(xn, wv_ref[...].astype(BF16))
    k32_ref[0] = k
    v32_ref[0] = v
    k16_ref[...] = k.astype(BF16)
    v16_ref[...] = v.astype(BF16)


def _kv(h, g, w_k, w_v, w_f, b_f, nb, tp):
    m, d = h.shape
    n = w_k.shape[1]
    nh = w_f.shape[1]
    tps = KV_TILES_PER_SEQ
    rows = m // nb // tps
    out32 = pl.BlockSpec((1, rows, KV_TILE), lambda i, j: (i // tps, i % tps, j))
    out16 = pl.BlockSpec((rows, KV_TILE), lambda i, j: (i, j))
    wspec = pl.BlockSpec((d, KV_TILE), lambda i, j: (0, j))
    return pl.pallas_call(
        _kv_kernel,
        grid=(nb * tps, n // KV_TILE),
        in_specs=[
            pl.BlockSpec((rows, d), lambda i, j: (i, 0)),
            pl.BlockSpec((1, d), lambda i, j: (0, 0)),
            wspec, wspec,
            pl.BlockSpec((nh, d), lambda i, j: (0, 0)),
            pl.BlockSpec((nh, 1), lambda i, j: (0, 0)),
        ],
        out_specs=[out32, out32, out16, out16, pl.BlockSpec((nh, rows), lambda i, j: (0, i))],
        out_shape=[
            jax.ShapeDtypeStruct((nb, tp, n), F32), jax.ShapeDtypeStruct((nb, tp, n), F32),
            jax.ShapeDtypeStruct((m, n), BF16), jax.ShapeDtypeStruct((m, n), BF16),
            jax.ShapeDtypeStruct((nh, m), F32),
        ],
        scratch_shapes=[pltpu.VMEM((rows, d), BF16)],
        compiler_params=_params(("parallel", "arbitrary")),
        name="kv_proj",
    )(h, g.reshape(1, d), w_k, w_v, w_f.T, b_f.reshape(nh, 1))


def _kv_rows_kernel(x_ref, g_ref, wk_ref, wv_ref, k_ref, v_ref):
    xn = _rms(x_ref[...], g_ref[...]).astype(BF16)
    k_ref[...] = _dot(xn, wk_ref[...].astype(BF16))
    v_ref[...] = _dot(xn, wv_ref[...].astype(BF16))


def _kv_rows(x, g, w_k, w_v):
    r, d = x.shape
    n = w_k.shape[1]
    wspec = pl.BlockSpec((d, PROJ_TILE), lambda j: (0, j))
    out = pl.BlockSpec((r, PROJ_TILE), lambda j: (0, j))
    return pl.pallas_call(
        _kv_rows_kernel,
        grid=(n // PROJ_TILE,),
        in_specs=[pl.BlockSpec((r, d), lambda j: (0, 0)), pl.BlockSpec((1, d), lambda j: (0, 0)), wspec, wspec],
        out_specs=[out, out],
        out_shape=[jax.ShapeDtypeStruct((r, n), F32), jax.ShapeDtypeStruct((r, n), F32)],
        compiler_params=_params(("parallel",)),
        name="kv_rows",
    )(x, g.reshape(1, d), w_k, w_v)


def _ln_proj_res_kernel(x_ref, lg_ref, lb_ref, w_ref, b_ref, r_ref, o_ref, wb_ref):
    _cast_weight_once(w_ref, wb_ref)
    x = x_ref[...]
    mu = jnp.mean(x, axis=-1, keepdims=True)
    xc = x - mu
    var = jnp.mean(xc * xc, axis=-1, keepdims=True)
    y = xc * lax.rsqrt(var + LN_EPS) * lg_ref[...] + lb_ref[...]
    xn = (y * jax.nn.sigmoid(y)).astype(BF16)
    o_ref[...] = r_ref[...] + _dot(xn, wb_ref[...]) + b_ref[...]


def _ln_proj_res(x, ln_g, ln_b, w, b, res):
    m, d = x.shape
    n = w.shape[1]
    w_spec, w_scratch = _resident_weight_specs(d, n)
    tile = pl.BlockSpec((SQ_ROW_TILE, n), lambda i: (i, 0))
    return pl.pallas_call(
        _ln_proj_res_kernel,
        grid=(m // SQ_ROW_TILE,),
        in_specs=[
            pl.BlockSpec((SQ_ROW_TILE, d), lambda i: (i, 0)),
            pl.BlockSpec((1, d), lambda i: (0, 0)),
            pl.BlockSpec((1, d), lambda i: (0, 0)),
            w_spec,
            pl.BlockSpec((1, n), lambda i: (0, 0)),
            tile,
        ],
        out_specs=tile,
        out_shape=jax.ShapeDtypeStruct((m, n), F32),
        scratch_shapes=[w_scratch],
        compiler_params=_params(("arbitrary",)),
        name="ln_proj_res",
    )(x, ln_g.reshape(1, d), ln_b.reshape(1, d), w, b.reshape(1, n), res)


def _proj_res_kernel(x_ref, w_ref, r_ref, o_ref, wb_ref):
    _cast_weight_once(w_ref, wb_ref)
    o_ref[...] = r_ref[...] + _dot(x_ref[...], wb_ref[...])


def _proj_res(x, w, res):
    m, d = x.shape
    n = w.shape[1]
    w_spec, w_scratch = _resident_weight_specs(d, n)
    tile = pl.BlockSpec((SQ_ROW_TILE, n), lambda i: (i, 0))
    return pl.pallas_call(
        _proj_res_kernel,
        grid=(m // SQ_ROW_TILE,),
        in_specs=[pl.BlockSpec((SQ_ROW_TILE, d), lambda i: (i, 0)), w_spec, tile],
        out_specs=tile,
        out_shape=jax.ShapeDtypeStruct((m, n), F32),
        scratch_shapes=[w_scratch],
        compiler_params=_params(("arbitrary",)),
        name="proj_res",
    )(x, w, res)


def _rms_out_kernel(x_ref, g_ref, o_ref):
    o_ref[0] = _rms(x_ref[...], g_ref[...])


def _rms_out(h, g, nb, seq, tpad, first):
    m, d = h.shape
    return pl.pallas_call(
        _rms_out_kernel,
        grid=(nb, seq // OUT_TILE),
        in_specs=[
            pl.BlockSpec((pl.Element(OUT_TILE), pl.Element(d)),
                         lambda b, i: ((b * (tpad // 8) + first // 8 + i * (OUT_TILE // 8)) * 8, 0)),
            pl.BlockSpec((1, d), lambda b, i: (0, 0)),
        ],
        out_specs=pl.BlockSpec((1, OUT_TILE, d), lambda b, i: (b, i, 0)),
        out_shape=jax.ShapeDtypeStruct((nb, seq, d), F32),
        compiler_params=_params(("parallel", "parallel")),
        name="rms_out",
    )(h, g.reshape(1, d))


def _rms_rows_kernel(x_ref, g_ref, o_ref):
    o_ref[...] = _rms(x_ref[...], g_ref[...])


def _rms_rows(x, g):
    r, d = x.shape
    blk = pl.BlockSpec((r, d), lambda i: (0, 0))
    return pl.pallas_call(
        _rms_rows_kernel,
        grid=(1,),
        in_specs=[blk, pl.BlockSpec((1, d), lambda i: (0, 0))],
        out_specs=blk,
        out_shape=jax.ShapeDtypeStruct((r, d), F32),
        compiler_params=_params(("arbitrary",)),
        name="rms_rows",
    )(x, g.reshape(1, d))


def _dwconv_kernel(x_ref, w_ref, b_ref, o_ref, buf_ref, *, width):
    t_len = x_ref.shape[1]
    lanes = x_ref.shape[2]
    lead = 33 - width
    buf_ref[pl.ds(0, 32), :] = jnp.zeros((32, lanes), F32)
    buf_ref[pl.ds(32, t_len), :] = x_ref[0]
    bias = b_ref[...]

    def chunk(c, carry):
        t0 = pl.multiple_of(c * CONV_CHUNK, 8)
        win = buf_ref[pl.ds(t0, CONV_CHUNK + 32), :]
        acc = jnp.broadcast_to(bias, (CONV_CHUNK, lanes))
        rows = CONV_CHUNK + 32
        for r in range(8):
            shift = r + lead
            z = win if shift == 0 else pltpu.roll(win, rows - shift, axis=0)
            for a, j in enumerate(range(r, width, 8)):
                acc = acc + w_ref[pl.ds(j, 1), :] * z[8 * a:8 * a + CONV_CHUNK]
        o_ref[0, pl.ds(t0, CONV_CHUNK), :] = acc
        return carry

    lax.fori_loop(0, t_len // CONV_CHUNK, chunk, 0)


def _dwconv(glu3, dw_w, dw_b):
    nb, t_len, d = glu3.shape
    width = dw_w.shape[0]
    blk = pl.BlockSpec((1, t_len, CONV_LANES), lambda b, c: (b, 0, c))
    return pl.pallas_call(
        functools.partial(_dwconv_kernel, width=width),
        grid=(nb, d // CONV_LANES),
        in_specs=[
            blk,
            pl.BlockSpec((width, CONV_LANES), lambda b, c: (0, c)),
            pl.BlockSpec((1, CONV_LANES), lambda b, c: (0, c)),
        ],
        out_specs=blk,
        out_shape=jax.ShapeDtypeStruct(glu3.shape, F32),
        scratch_shapes=[pltpu.VMEM((t_len + 32, CONV_LANES), F32)],
        compiler_params=_params(("parallel", "parallel")),
        name="dwconv",
    )(glu3, dw_w, dw_b.reshape(1, d))


def _dwconv_step_kernel(s_ref, x_ref, w_ref, b_ref, y_ref, ns_ref, buf_ref, *, width):
    n_state = width - 1
    n_new = x_ref.shape[1]
    d = x_ref.shape[2]
    buf_ref[pl.ds(0, n_state), :] = s_ref[0]
    buf_ref[pl.ds(n_state, n_new), :] = x_ref[0]
    acc = jnp.broadcast_to(b_ref[...], (n_new, d))
    for j in range(width):
        acc = acc + w_ref[pl.ds(j, 1), :] * buf_ref[pl.ds(j, n_new), :]
    y_ref[0] = acc
    ns_ref[0] = buf_ref[pl.ds(n_new, n_state), :]


def _dwconv_step(state, glu_s, dw_w, dw_b):
    nb, n_state, d = state.shape
    n_new = glu_s.shape[1]
    width = dw_w.shape[0]
    return pl.pallas_call(
        functools.partial(_dwconv_step_kernel, width=width),
        grid=(nb,),
        in_specs=[
            pl.BlockSpec((1, n_state, d), lambda b: (b, 0, 0)),
            pl.BlockSpec((1, n_new, d), lambda b: (b, 0, 0)),
            pl.BlockSpec((width, d), lambda b: (0, 0)),
            pl.BlockSpec((1, d), lambda b: (0, 0)),
        ],
        out_specs=[
            pl.BlockSpec((1, n_new, d), lambda b: (b, 0, 0)),
            pl.BlockSpec((1, n_state, d), lambda b: (b, 0, 0)),
        ],
        out_shape=[
            jax.ShapeDtypeStruct((nb, n_new, d), F32),
            jax.ShapeDtypeStruct((nb, n_state, d), F32),
        ],
        scratch_shapes=[pltpu.VMEM((n_state + n_new + 2, d), F32)],
        compiler_params=_params(("parallel",)),
        name="dwconv_step",
    )(state, glu_s, dw_w, dw_b.reshape(1, d))


def _neg_cumsum_kernel(lf_ref, o_ref, carry_ref):
    @pl.when(pl.program_id(1) == 0)
    def _():
        carry_ref[...] = jnp.zeros_like(carry_ref)

    row = lax.broadcasted_iota(jnp.int32, (LANES, LANES), 0)
    col = lax.broadcasted_iota(jnp.int32, (LANES, LANES), 1)
    upper = (row <= col).astype(BF16)
    hi, mid, lo = _split3(lf_ref[...])
    cs = (_dot(lo, upper) + _dot(mid, upper)) + _dot(hi, upper) + carry_ref[...]
    o_ref[...] = cs * -LOG2E
    carry_ref[...] = jnp.broadcast_to(cs[:, LANES - 1:LANES], cs.shape)


def _neg_cumsum(lft, nb):
    nh, m = lft.shape
    steps = m // nb // LANES
    blk = pl.BlockSpec((nh, LANES), lambda b, j: (0, b * steps + j))
    return pl.pallas_call(
        _neg_cumsum_kernel,
        grid=(nb, steps),
        in_specs=[blk],
        out_specs=blk,
        out_shape=jax.ShapeDtypeStruct((nh, m), F32),
        scratch_shapes=[pltpu.VMEM((nh, LANES), F32)],
        compiler_params=_params(("arbitrary", "arbitrary")),
        name="neg_cumsum",
    )(lft)


def _softmax_step(s, m_i, l_i, acc, v):
    m_new = jnp.maximum(m_i, jnp.max(s, axis=-1, keepdims=True))
    p = jnp.exp2(s - m_new)
    alpha = jnp.exp2(m_i - m_new)
    l_new = alpha * l_i + jnp.sum(p, axis=-1, keepdims=True)
    acc_new = alpha * acc + _dot(p.astype(BF16), v)
    return m_new, l_new, acc_new


def _attn_kernel(q_ref, k_ref, v_ref, nc_ref, o_ref, s_ref):
    qi = pl.program_id(2)
    tq = q_ref.shape[0]
    dh = q_ref.shape[1] // ATT_HEADS
    win = 2 * ATT_TILE
    heads = range(ATT_HEADS)

    def lanes(g):
        return slice(g * dh, (g + 1) * dh)

    def window(kj):
        return pl.ds(pl.multiple_of(kj * ATT_TILE, ATT_TILE), win)

    q = [q_ref[:, lanes(g)] for g in heads]

    def scores(kj):
        k = k_ref[window(kj), :]
        return [_dot_nt(q[g], k[:, lanes(g)]) + jnp.concatenate([nc_ref[g, kj], nc_ref[g, kj + 1]], axis=1)
                for g in heads]

    def store_scores(kj):
        for g, s in enumerate(scores(kj)):
            s_ref[g] = s

    full = lax.shift_right_logical(qi, 1)
    last = jnp.maximum(qi - 1, 0)
    store_scores(jnp.where(full > 0, 0, last))

    def body(p, carry):
        s = [s_ref[g] for g in heads]
        store_scores(jnp.where(p + 1 < full, 2 * (p + 1), last))
        v = v_ref[window(2 * p), :]
        return tuple(_softmax_step(s[g], *carry[g], v[:, lanes(g)]) for g in heads)

    init = (jnp.full((tq, 1), NEG_INF, F32), jnp.zeros((tq, 1), F32), jnp.zeros((tq, dh), F32))
    carry = lax.fori_loop(0, full, body, (init,) * ATT_HEADS)

    row = lax.broadcasted_iota(jnp.int32, (tq, win), 0) + qi * ATT_TILE
    col = lax.broadcasted_iota(jnp.int32, (tq, win), 1) + last * ATT_TILE
    keep = (col <= row) & (col >= 2 * full * ATT_TILE)
    v = v_ref[window(last), :]
    for g in heads:
        _, l_i, acc = _softmax_step(jnp.where(keep, s_ref[g], NEG_INF), *carry[g], v[:, lanes(g)])
        o_ref[:, lanes(g)] = (acc / l_i).astype(o_ref.dtype)


def _attention(q, k, v, negc, nb, nh):
    m, hd = q.shape
    dh = hd // nh
    t = m // nb
    nt = t // ATT_TILE
    assert t % ATT_TILE == 0 and nt >= 2 and nh % ATT_HEADS == 0
    negc4 = negc.reshape(nh, nb, nt, 1, ATT_TILE)
    gw = ATT_HEADS * dh
    return pl.pallas_call(
        _attn_kernel,
        grid=(nb, nh // ATT_HEADS, nt),
        in_specs=[
            pl.BlockSpec((ATT_TILE, gw), lambda b, h, i: (b * nt + i, h)),
            pl.BlockSpec((t, gw), lambda b, h, i: (b, h)),
            pl.BlockSpec((t, gw), lambda b, h, i: (b, h)),
            pl.BlockSpec((ATT_HEADS, None, nt, 1, ATT_TILE), lambda b, h, i: (h, b, 0, 0, 0)),
        ],
        out_specs=pl.BlockSpec((ATT_TILE, gw), lambda b, h, i: (b * nt + i, h)),
        out_shape=jax.ShapeDtypeStruct((m, hd), BF16),
        scratch_shapes=[pltpu.VMEM((ATT_HEADS, ATT_TILE, 2 * ATT_TILE), F32)],
        compiler_params=_params(("parallel", "parallel", "arbitrary")),
        name="fox_prompt",
    )(q, k, v, negc4)


def _sample_bias_kernel(pt_ref, lf_ref, lfn_ref, o_ref, g_ref, low_ref, *, n_pages, nh):
    b = pl.program_id(0)
    rpp = lfn_ref.shape[1]
    n_rows = g_ref.shape[0]

    @pl.when(b == 0)
    def _():
        r = lax.broadcasted_iota(jnp.int32, (n_rows, n_rows), 0)
        c = lax.broadcasted_iota(jnp.int32, (n_rows, n_rows), 1)
        low_ref[...] = (c < r).astype(BF16)
        g_ref[...] = jnp.zeros_like(g_ref)

    def gather(p, carry):
        src = pl.multiple_of(pt_ref[b * n_pages + p] * rpp, rpp)
        dst = pl.multiple_of(p * rpp, rpp)
        g_ref[pl.ds(dst, rpp), :] = lf_ref[pl.ds(src, rpp), :]
        return carry

    lax.fori_loop(0, n_pages, gather, 0)
    g_ref[pl.ds(n_pages * rpp, rpp), :] = lfn_ref[0]

    li = lax.broadcasted_iota(jnp.int32, (BIAS_LANES, BIAS_LANES), 0)
    lj = lax.broadcasted_iota(jnp.int32, (BIAS_LANES, BIAS_LANES), 1)
    same_head = (li & (nh - 1)) == (lj & (nh - 1))
    t_row = same_head.astype(BF16)
    t_pre = (same_head & (li <= lj)).astype(BF16)
    hi, mid, lo = _split3(g_ref[...])
    within = (_dot(lo, t_pre) + _dot(mid, t_pre)) + _dot(hi, t_pre)
    row_tot = (_dot(lo, t_row) + _dot(mid, t_row)) + _dot(hi, t_row)
    hi, mid, lo = _split3(row_tot)
    low = low_ref[...]
    before = (_dot(low, lo) + _dot(low, mid)) + _dot(low, hi)
    o_ref[0] = (within + before) * -LOG2E


def _sample_bias(page_table, lf_tab, lf_new, nh):
    ns, n_pages = page_table.shape
    rpp = lf_new.shape[1]
    n_rows = -(-((n_pages + 1) * rpp) // 16) * 16
    grid_spec = pltpu.PrefetchScalarGridSpec(
        num_scalar_prefetch=1,
        grid=(ns,),
        in_specs=[
            _single(lf_tab.shape, lambda b, pt: (0, 0)),
            pl.BlockSpec((1, rpp, BIAS_LANES), lambda b, pt: (b, 0, 0)),
        ],
        out_specs=pl.BlockSpec((1, n_rows, BIAS_LANES), lambda b, pt: (b, 0, 0)),
        scratch_shapes=[pltpu.VMEM((n_rows, BIAS_LANES), F32), pltpu.VMEM((n_rows, n_rows), BF16)],
    )
    return pl.pallas_call(
        functools.partial(_sample_bias_kernel, n_pages=n_pages, nh=nh),
        grid_spec=grid_spec,
        out_shape=jax.ShapeDtypeStruct((ns, n_rows, BIAS_LANES), F32),
        compiler_params=_params(("arbitrary",)),
        name="sample_bias",
    )(page_table.reshape(-1), lf_tab, lf_new)


def _paged_attn_kernel(pt_ref, q_ref, *refs, nh, n_new):
    pps = PAGES_PER_STEP
    k_refs, v_refs = refs[:pps], refs[pps:2 * pps]
    nc_ref, kn_ref, vn_ref, o_ref, qm_ref, mask_ref, m_ref, l_ref, acc_ref = refs[2 * pps:]
    p = pl.program_id(1)
    n_steps = pl.num_programs(1)
    rows, dh = qm_ref.shape
    cols = kn_ref.shape[0]
    rpp = cols // BIAS_LANES
    q_shift = n_new.bit_length() - 1
    h_shift = nh.bit_length() - 1

    @pl.when(p == 0)
    def _():
        q = q_ref[0].astype(F32)
        qm_ref[...] = jnp.concatenate([q[:, h * dh:(h + 1) * dh] for h in range(nh)], axis=0).astype(BF16)
        r = lax.broadcasted_iota(jnp.int32, (rows, cols), 0)
        c = lax.broadcasted_iota(jnp.int32, (rows, cols), 1)
        mask_ref[...] = jnp.where((r >> q_shift) == (c & (nh - 1)), 0.0, NEG_INF)
        m_ref[...] = jnp.full_like(m_ref, NEG_INF)
        l_ref[...] = jnp.zeros_like(l_ref)
        acc_ref[...] = jnp.zeros_like(acc_ref)

    def update(ks, vs, bias_row0, causal):
        qm = qm_ref[...]
        logits = [_dot_nt(qm, k) for k in ks]
        state = (m_ref[...], l_ref[...], acc_ref[...])
        for j, (s, v) in enumerate(zip(logits, vs)):
            bias = nc_ref[0, pl.ds(bias_row0 + j * rpp, rpp), :]
            s = jnp.concatenate([s[:, i * BIAS_LANES:(i + 1) * BIAS_LANES] + bias[i:i + 1, :]
                                 for i in range(rpp)], axis=1) + mask_ref[...]
            if causal:
                qpos = lax.broadcasted_iota(jnp.int32, s.shape, 0) & (n_new - 1)
                kpos = lax.broadcasted_iota(jnp.int32, s.shape, 1) >> h_shift
                s = jnp.where(kpos <= qpos, s, NEG_INF)
            state = _softmax_step(s, *state, v)
        m_ref[...], l_ref[...], acc_ref[...] = state

    update([r[...].astype(BF16) for r in k_refs], [r[...].astype(BF16) for r in v_refs],
           pl.multiple_of(p * (pps * rpp), rpp), False)

    @pl.when(p == n_steps - 1)
    def _():
        update([kn_ref[...]], [vn_ref[...]], n_steps * pps * rpp, True)
        o = acc_ref[...] / l_ref[...]
        for h in range(nh):
            o_ref[0, :, h * dh:(h + 1) * dh] = o[h * n_new:(h + 1) * n_new, :]


def _paged_attention(page_table, q_s, k_rows, v_rows, negc, k_new, v_new, nh):
    ns, n_new, hd = q_s.shape
    dh = hd // nh
    n_pages = page_table.shape[1]
    cols = k_new.shape[0] // ns
    rows = nh * n_new
    pps = PAGES_PER_STEP
    assert n_new & (n_new - 1) == 0 and nh & (nh - 1) == 0 and cols % BIAS_LANES == 0
    assert n_pages % pps == 0

    def cache(j):
        return pl.BlockSpec((cols, dh), lambda b, p, pt: (pt[b * n_pages + p * pps + j], 0))

    caches = [cache(j) for j in range(pps)]
    new = pl.BlockSpec((cols, dh), lambda b, p, pt: (b, 0))
    per_b = lambda shape: pl.BlockSpec(shape, lambda b, p, pt: (b, 0, 0))
    grid_spec = pltpu.PrefetchScalarGridSpec(
        num_scalar_prefetch=1,
        grid=(ns, n_pages // pps),
        in_specs=[per_b((1, n_new, hd))] + caches + caches + [per_b((1,) + negc.shape[1:]), new, new],
        out_specs=per_b((1, n_new, hd)),
        scratch_shapes=[
            pltpu.VMEM((rows, dh), BF16),
            pltpu.VMEM((rows, cols), F32),
            pltpu.VMEM((rows, 1), F32),
            pltpu.VMEM((rows, 1), F32),
            pltpu.VMEM((rows, dh), F32),
        ],
    )
    return pl.pallas_call(
        functools.partial(_paged_attn_kernel, nh=nh, n_new=n_new),
        grid_spec=grid_spec,
        out_shape=jax.ShapeDtypeStruct((ns, n_new, hd), F32),
        compiler_params=_params(("parallel", "arbitrary")),
        name="fox_sample",
    )(page_table.reshape(-1), q_s, *([k_rows] * pps), *([v_rows] * pps), negc, k_new, v_new)


def kernel(x_prompt, x_sample, cache_k, cache_v, cache_logf, state_conv, page_table, meta_tokens, norm_ffn1, ffn1_w1, ffn1_w3, ffn1_w2, norm_mix, conv_w_in, conv_b_in, conv_dw_w, conv_dw_b, conv_ln_g, conv_ln_b, conv_w_out, conv_b_out, norm_kv, w_k, w_v, w_f, b_f, w_q, w_o, norm_ffn2, ffn2_w1, ffn2_w3, ffn2_w2, norm_final):
    nb, seq, d = x_prompt.shape
    ns, s_len, _ = x_sample.shape
    n_meta = meta_tokens.shape[0]
    nh = w_f.shape[1]
    dh = w_k.shape[1] // nh
    n_pool, page = cache_k.shape[0], cache_k.shape[1]
    width = conv_dw_w.shape[1]
    depth = norm_ffn1.shape[0]
    n_a = conv_w_in.shape[0]

    tp = seq + n_meta
    tpad = -(-(tp + ns * s_len) // ATT_TILE) * ATT_TILE
    m = nb * tpad
    s0 = (nb - 1) * tpad + tp
    n_s = ns * s_len
    assert m % ROW_TILE == 0 and m % SQ_ROW_TILE == 0
    assert tpad % (KV_TILES_PER_SEQ * LANES) == 0 and tpad % CONV_CHUNK == 0
    assert width <= 33 and d % CONV_LANES == 0 and seq % OUT_TILE == 0 and n_meta % 8 == 0
    assert (page * nh) % BIAS_LANES == 0 and BIAS_LANES % nh == 0
    assert s0 % 16 == 0 and tpad - tp >= n_s and depth == 2 * n_a == 2

    meta = meta_tokens.astype(F32)
    pieces = []
    for b in range(nb):
        pieces += [meta, x_prompt[b]]
        if b < nb - 1:
            pieces.append(jnp.zeros((tpad - tp, d), F32))
    pieces += [x_sample.reshape(n_s, d), jnp.zeros((tpad - tp - n_s, d), F32)]
    h = jnp.concatenate(pieces, axis=0)

    def prompt_rows(x):
        return x.reshape(nb, tpad, x.shape[-1])[:, :tp]

    def sample_rows(x):
        return x[s0:s0 + n_s]

    h = _ffn(h, norm_ffn1, ffn1_w1, ffn1_w3, ffn1_w2, 0)
    glu = _rms_glu(h, norm_mix[0], conv_w_in[0], conv_b_in[0])
    y = _dwconv(glu.reshape(nb, tpad, d), conv_dw_w[0], conv_dw_b[0]).reshape(m, d)
    y_s, conv_s = _dwconv_step(state_conv[0], sample_rows(glu).reshape(ns, s_len, d),
                               conv_dw_w[0], conv_dw_b[0])
    y = lax.dynamic_update_slice(y, y_s.reshape(n_s, d), (s0, 0))
    conv_prompt = prompt_rows(glu)[None, :, tp - (width - 1):]
    conv_sample = conv_s[None]
    h = _ln_proj_res(y, conv_ln_g[0], conv_ln_b[0], conv_w_out[0], conv_b_out[0], h)
    h = _ffn(h, norm_ffn2, ffn2_w1, ffn2_w3, ffn2_w2, 0)

    k_prompt, v_prompt, k16, v16, lft = _kv(h, norm_kv, w_k, w_v, w_f, b_f, nb, tp)
    k_s, v_s = _kv_rows(sample_rows(h), norm_kv, w_k, w_v)
    h = _ffn(h, norm_ffn1, ffn1_w1, ffn1_w3, ffn1_w2, 1)
    q = _rms_proj(h, norm_mix[1], w_q[0], dh ** -0.5 * LOG2E, BF16)
    negc = _neg_cumsum(lft, nb)
    o = _attention(q, k16, v16, negc, nb, nh)

    def new_page(x):
        x = x.reshape(ns, s_len, x.shape[-1])
        return jnp.pad(x, ((0, 0), (0, page - s_len), (0, 0)))

    lf_rows = lft.T
    rpp = page * nh // BIAS_LANES
    negc_s = _sample_bias(page_table, cache_logf.reshape(n_pool * rpp, BIAS_LANES),
                          new_page(sample_rows(lf_rows)).reshape(ns, rpp, BIAS_LANES), nh)
    o_s = _paged_attention(
        page_table, sample_rows(q).reshape(ns, s_len, nh * dh),
        cache_k.reshape(n_pool * page * nh, dh), cache_v.reshape(n_pool * page * nh, dh), negc_s,
        new_page(sample_rows(k16)).reshape(ns * page * nh, dh),
        new_page(sample_rows(v16)).reshape(ns * page * nh, dh), nh)
    o = lax.dynamic_update_slice(o, o_s.reshape(n_s, nh * dh).astype(BF16), (s0, 0))
    h = _proj_res(o, w_o[0], h)
    h = _ffn(h, norm_ffn2, ffn2_w1, ffn2_w3, ffn2_w2, 1)

    y_prompt = _rms_out(h, norm_final, nb, seq, tpad, n_meta)
    y_sample = _rms_rows(sample_rows(h), norm_final).reshape(ns, s_len, d)
    k_prompt = k_prompt.reshape(nb, tp, nh, dh)
    v_prompt = v_prompt.reshape(nb, tp, nh, dh)
    logf_prompt = prompt_rows(lf_rows)
    k_sample = k_s.reshape(ns, s_len, nh, dh)
    v_sample = v_s.reshape(ns, s_len, nh, dh)
    logf_sample = sample_rows(lf_rows).reshape(ns, s_len, nh)
    return (y_prompt, y_sample, k_prompt, v_prompt, logf_prompt, conv_prompt,
            k_sample, v_sample, logf_sample, conv_sample)
```

```python
import functools

import jax
import jax.numpy as jnp
from jax import lax
from jax.experimental import pallas as pl
from jax.experimental.pallas import tpu as pltpu

F32 = jnp.float32
BF16 = jnp.bfloat16

RMS_EPS = 1e-6
LN_EPS = 1e-5
NEG_INF = -1e30
FFN_RESIDUAL = 0.5

LANES = 128
ROW_TILE = 1056
FFN_ROWS = 2112
FFN_CHUNKS = 4
FFN_TILE = 256
PROJ_TILE = 512
SQ_ROW_TILE = 528
KV_TILES_PER_SEQ = 3
KV_TILE = 256
OUT_TILE = 512
ATT_TILE = 384
ATT_HEADS = 4
BIAS_LANES = 256
PAGES_PER_STEP = 4
LOG2E = 1.4426950408889634
CONV_CHUNK = 264
CONV_LANES = 256
VMEM_LIMIT = 56 * 1024 * 1024


def _params(semantics):
    return pltpu.CompilerParams(dimension_semantics=semantics, vmem_limit_bytes=VMEM_LIMIT)


def _single(shape, index_map):
    return pl.BlockSpec(shape, index_map, pipeline_mode=pl.Buffered(1))


def _rms(x, g):
    ms = jnp.mean(x * x, axis=-1, keepdims=True)
    return x * lax.rsqrt(ms + RMS_EPS) * g


def _split3(x):
    hi = x.astype(BF16)
    r1 = x - hi.astype(F32)
    mid = r1.astype(BF16)
    lo = (r1 - mid.astype(F32)).astype(BF16)
    return hi, mid, lo


def _dot(a, b):
    return jnp.dot(a, b, preferred_element_type=F32)


def _dot_nt(a, b):
    return lax.dot_general(a, b, (((1,), (1,)), ((), ())), preferred_element_type=F32)


def _ffn_kernel(x_hbm, g_ref, w1_ref, w3_ref, w2_ref, o_hbm, acc_ref, xn_ref, gate_ref, sem):
    i = pl.program_id(0)
    f = pl.program_id(1)
    last = pl.num_programs(1) - 1
    tile = acc_ref.shape[0]
    chunk = tile // FFN_CHUNKS

    def rows(c):
        return pl.ds(c * chunk, chunk)

    def chunk_copy(c, store):
        hbm = (o_hbm if store else x_hbm).at[pl.ds(pl.multiple_of(i * tile + c * chunk, 8), chunk)]
        vmem = acc_ref.at[rows(c)]
        return pltpu.make_async_copy(vmem, hbm, sem.at[c]) if store else pltpu.make_async_copy(hbm, vmem, sem.at[c])

    def gate():
        xn = xn_ref[...]
        a = _dot(xn, w1_ref[...].astype(BF16))
        b = _dot(xn, w3_ref[...].astype(BF16))
        return (FFN_RESIDUAL * a * jax.nn.sigmoid(a) * b).astype(BF16)

    @pl.when(f == 0)
    def _():
        for c in range(FFN_CHUNKS):
            chunk_copy(c, False).start()
        for c in range(FFN_CHUNKS):
            chunk_copy(c, False).wait()
            xn_ref[rows(c), :] = _rms(acc_ref[rows(c), :], g_ref[...]).astype(BF16)
        gate_ref[...] = gate()

    @pl.when((f > 0) & (f < last))
    def _():
        prev = gate_ref[...]
        gate_ref[...] = gate()
        acc_ref[...] += _dot(prev, w2_ref[...].astype(BF16))

    @pl.when(f == last)
    def _():
        w2 = w2_ref[...].astype(BF16)
        for c in range(FFN_CHUNKS):
            acc_ref[rows(c), :] += _dot(gate_ref[rows(c), :], w2)
            chunk_copy(c, True).start()
        for c in range(FFN_CHUNKS):
            chunk_copy(c, True).wait()


def _ffn(h, g, w1, w3, w2, layer):
    m, d = h.shape
    dff = w1.shape[2]
    nf = dff // FFN_TILE
    assert m % FFN_ROWS == 0 and FFN_ROWS % (16 * FFN_CHUNKS) == 0
    return pl.pallas_call(
        _ffn_kernel,
        grid=(m // FFN_ROWS, nf + 1),
        in_specs=[
            pl.BlockSpec(memory_space=pl.ANY),
            pl.BlockSpec((None, 1, d), lambda i, f: (layer, 0, 0)),
            pl.BlockSpec((None, d, FFN_TILE), lambda i, f: (layer, 0, jnp.minimum(f, nf - 1))),
            pl.BlockSpec((None, d, FFN_TILE), lambda i, f: (layer, 0, jnp.minimum(f, nf - 1))),
            pl.BlockSpec((None, FFN_TILE, d), lambda i, f: (layer, jnp.maximum(f - 1, 0), 0)),
        ],
        out_specs=pl.BlockSpec(memory_space=pl.ANY),
        out_shape=jax.ShapeDtypeStruct((m, d), F32),
        scratch_shapes=[
            pltpu.VMEM((FFN_ROWS, d), F32),
            pltpu.VMEM((FFN_ROWS, d), BF16),
            pltpu.VMEM((FFN_ROWS, FFN_TILE), BF16),
            pltpu.SemaphoreType.DMA((FFN_CHUNKS,)),
        ],
        compiler_params=_params(("arbitrary", "arbitrary")),
        name="ffn",
    )(h, g.reshape(g.shape[0], 1, d), w1, w3, w2)


def _cast_weight_once(w_ref, wb_ref):
    @pl.when(pl.program_id(0) == 0)
    def _():
        wb_ref[...] = w_ref[...].astype(BF16)


def _resident_weight_specs(d, n):
    return _single((d, n), lambda i: (0, 0)), pltpu.VMEM((d, n), BF16)


def _rms_proj_kernel(x_ref, g_ref, w_ref, o_ref, wb_ref, *, scale):
    _cast_weight_once(w_ref, wb_ref)
    xn = _rms(x_ref[...], g_ref[...]).astype(BF16)
    o_ref[...] = (_dot(xn, wb_ref[...]) * scale).astype(o_ref.dtype)


def _rms_proj(h, g, w, scale, out_dtype):
    m, d = h.shape
    n = w.shape[1]
    w_spec, w_scratch = _resident_weight_specs(d, n)
    return pl.pallas_call(
        functools.partial(_rms_proj_kernel, scale=scale),
        grid=(m // SQ_ROW_TILE,),
        in_specs=[
            pl.BlockSpec((SQ_ROW_TILE, d), lambda i: (i, 0)),
            pl.BlockSpec((1, d), lambda i: (0, 0)),
            w_spec,
        ],
        out_specs=pl.BlockSpec((SQ_ROW_TILE, n), lambda i: (i, 0)),
        out_shape=jax.ShapeDtypeStruct((m, n), out_dtype),
        scratch_shapes=[w_scratch],
        compiler_params=_params(("arbitrary",)),
        name="rms_proj",
    )(h, g.reshape(1, d), w)


def _rms_glu_kernel(x_ref, g_ref, wa_ref, wg_ref, ba_ref, bg_ref, o_ref, xn_ref):
    @pl.when(pl.program_id(1) == 0)
    def _():
        xn_ref[...] = _rms(x_ref[...], g_ref[...]).astype(BF16)

    xn = xn_ref[...]
    a = _dot(xn, wa_ref[...].astype(BF16)) + ba_ref[...]
    gt = _dot(xn, wg_ref[...].astype(BF16)) + bg_ref[...]
    o_ref[...] = a * jax.nn.sigmoid(gt)


def _rms_glu(h, g, w_in, b_in):
    m, d = h.shape
    nt = d // PROJ_TILE
    return pl.pallas_call(
        _rms_glu_kernel,
        grid=(m // ROW_TILE, nt),
        in_specs=[
            pl.BlockSpec((ROW_TILE, d), lambda i, j: (i, 0)),
            pl.BlockSpec((1, d), lambda i, j: (0, 0)),
            pl.BlockSpec((d, PROJ_TILE), lambda i, j: (0, j)),
            pl.BlockSpec((d, PROJ_TILE), lambda i, j: (0, j + nt)),
            pl.BlockSpec((1, PROJ_TILE), lambda i, j: (0, j)),
            pl.BlockSpec((1, PROJ_TILE), lambda i, j: (0, j + nt)),
        ],
        out_specs=pl.BlockSpec((ROW_TILE, PROJ_TILE), lambda i, j: (i, j)),
        out_shape=jax.ShapeDtypeStruct((m, d), F32),
        scratch_shapes=[pltpu.VMEM((ROW_TILE, d), BF16)],
        compiler_params=_params(("parallel", "arbitrary")),
        name="rms_glu",
    )(h, g.reshape(1, d), w_in, w_in, b_in.reshape(1, 2 * d), b_in.reshape(1, 2 * d))


def _log_sigmoid(z):
    return jnp.minimum(z, 0.0) - jnp.log1p(jnp.exp(-jnp.abs(z)))


def _kv_kernel(x_ref, g_ref, wk_ref, wv_ref, wft_ref, bf_ref,
               k32_ref, v32_ref, k16_ref, v16_ref, lft_ref, xn_ref):
    @pl.when(pl.program_id(1) == 0)
    def _():
        xn = _rms(x_ref[...], g_ref[...]).astype(BF16)
        xn_ref[...] = xn
        lft_ref[...] = _log_sigmoid(_dot_nt(wft_ref[...].astype(BF16), xn) + bf_ref[...])

    xn = xn_ref[...]
    k = _dot(xn, wk_ref[...].astype(BF16))
    v = _dot(xn, wv_ref[...].astype(BF16))
    k32_ref[0] = k
    v32_ref[0] = v
    k16_ref[...] = k.astype(BF16)
    v16_ref[...] = v.astype(BF16)


def _kv(h, g, w_k, w_v, w_f, b_f, nb, tp):
    m, d = h.shape
    n = w_k.shape[1]
    nh = w_f.shape[1]
    tps = KV_TILES_PER_SEQ
    rows = m // nb // tps
    out32 = pl.BlockSpec((1, rows, KV_TILE), lambda i, j: (i // tps, i % tps, j))
    out16 = pl.BlockSpec((rows, KV_TILE), lambda i, j: (i, j))
    wspec = pl.BlockSpec((d, KV_TILE), lambda i, j: (0, j))
    return pl.pallas_call(
        _kv_kernel,
        grid=(nb * tps, n // KV_TILE),
        in_specs=[
            pl.BlockSpec((rows, d), lambda i, j: (i, 0)),
            pl.BlockSpec((1, d), lambda i, j: (0, 0)),
            wspec, wspec,
            pl.BlockSpec((nh, d), lambda i, j: (0, 0)),
            pl.BlockSpec((nh, 1), lambda i, j: (0, 0)),
        ],
        out_specs=[out32, out32, out16, out16, pl.BlockSpec((nh, rows), lambda i, j: (0, i))],
        out_shape=[
            jax.ShapeDtypeStruct((nb, tp, n), F32), jax.ShapeDtypeStruct((nb, tp, n), F32),
            jax.ShapeDtypeStruct((m, n), BF16), jax.ShapeDtypeStruct((m, n), BF16),
            jax.ShapeDtypeStruct((nh, m), F32),
        ],
        scratch_shapes=[pltpu.VMEM((rows, d), BF16)],
        compiler_params=_params(("parallel", "arbitrary")),
        name="kv_proj",
    )(h, g.reshape(1, d), w_k, w_v, w_f.T, b_f.reshape(nh, 1))


def _kv_rows_kernel(x_ref, g_ref, wk_ref, wv_ref, k_ref, v_ref):
    xn = _rms(x_ref[...], g_ref[...]).astype(BF16)
    k_ref[...] = _dot(xn, wk_ref[...].astype(BF16))
    v_ref[...] = _dot(xn, wv_ref[...].astype(BF16))


def _kv_rows(x, g, w_k, w_v):
    r, d = x.shape
    n = w_k.shape[1]
    wspec = pl.BlockSpec((d, PROJ_TILE), lambda j: (0, j))
    out = pl.BlockSpec((r, PROJ_TILE), lambda j: (0, j))
    return pl.pallas_call(
        _kv_rows_kernel,
        grid=(n // PROJ_TILE,),
        in_specs=[pl.BlockSpec((r, d), lambda j: (0, 0)), pl.BlockSpec((1, d), lambda j: (0, 0)), wspec, wspec],
        out_specs=[out, out],
        out_shape=[jax.ShapeDtypeStruct((r, n), F32), jax.ShapeDtypeStruct((r, n), F32)],
        compiler_params=_params(("parallel",)),
        name="kv_rows",
    )(x, g.reshape(1, d), w_k, w_v)


def _ln_proj_res_kernel(x_ref, lg_ref, lb_ref, w_ref, b_ref, r_ref, o_ref, wb_ref):
    _cast_weight_once(w_ref, wb_ref)
    x = x_ref[...]
    mu = jnp.mean(x, axis=-1, keepdims=True)
    xc = x - mu
    var = jnp.mean(xc * xc, axis=-1, keepdims=True)
    y = xc * lax.rsqrt(var + LN_EPS) * lg_ref[...] + lb_ref[...]
    xn = (y * jax.nn.sigmoid(y)).astype(BF16)
    o_ref[...] = r_ref[...] + _dot(xn, wb_ref[...]) + b_ref[...]


def _ln_proj_res(x, ln_g, ln_b, w, b, res):
    m, d = x.shape
    n = w.shape[1]
    w_spec, w_scratch = _resident_weight_specs(d, n)
    tile = pl.BlockSpec((SQ_ROW_TILE, n), lambda i: (i, 0))
    return pl.pallas_call(
        _ln_proj_res_kernel,
        grid=(m // SQ_ROW_TILE,),
        in_specs=[
            pl.BlockSpec((SQ_ROW_TILE, d), lambda i: (i, 0)),
            pl.BlockSpec((1, d), lambda i: (0, 0)),
            pl.BlockSpec((1, d), lambda i: (0, 0)),
            w_spec,
            pl.BlockSpec((1, n), lambda i: (0, 0)),
            tile,
        ],
        out_specs=tile,
        out_shape=jax.ShapeDtypeStruct((m, n), F32),
        scratch_shapes=[w_scratch],
        compiler_params=_params(("arbitrary",)),
        name="ln_proj_res",
    )(x, ln_g.reshape(1, d), ln_b.reshape(1, d), w, b.reshape(1, n), res)


def _proj_res_kernel(x_ref, w_ref, r_ref, o_ref, wb_ref):
    _cast_weight_once(w_ref, wb_ref)
    o_ref[...] = r_ref[...] + _dot(x_ref[...], wb_ref[...])


def _proj_res(x, w, res):
    m, d = x.shape
    n = w.shape[1]
    w_spec, w_scratch = _resident_weight_specs(d, n)
    tile = pl.BlockSpec((SQ_ROW_TILE, n), lambda i: (i, 0))
    return pl.pallas_call(
        _proj_res_kernel,
        grid=(m // SQ_ROW_TILE,),
        in_specs=[pl.BlockSpec((SQ_ROW_TILE, d), lambda i: (i, 0)), w_spec, tile],
        out_specs=tile,
        out_shape=jax.ShapeDtypeStruct((m, n), F32),
        scratch_shapes=[w_scratch],
        compiler_params=_params(("arbitrary",)),
        name="proj_res",
    )(x, w, res)


def _rms_out_kernel(x_ref, g_ref, o_ref):
    o_ref[0] = _rms(x_ref[...], g_ref[...])


def _rms_out(h, g, nb, seq, tpad, first):
    m, d = h.shape
    return pl.pallas_call(
        _rms_out_kernel,
        grid=(nb, seq // OUT_TILE),
        in_specs=[
            pl.BlockSpec((pl.Element(OUT_TILE), pl.Element(d)),
                         lambda b, i: ((b * (tpad // 8) + first // 8 + i * (OUT_TILE // 8)) * 8, 0)),
            pl.BlockSpec((1, d), lambda b, i: (0, 0)),
        ],
        out_specs=pl.BlockSpec((1, OUT_TILE, d), lambda b, i: (b, i, 0)),
        out_shape=jax.ShapeDtypeStruct((nb, seq, d), F32),
        compiler_params=_params(("parallel", "parallel")),
        name="rms_out",
    )(h, g.reshape(1, d))


def _rms_rows_kernel(x_ref, g_ref, o_ref):
    o_ref[...] = _rms(x_ref[...], g_ref[...])


def _rms_rows(x, g):
    r, d = x.shape
    blk = pl.BlockSpec((r, d), lambda i: (0, 0))
    return pl.pallas_call(
        _rms_rows_kernel,
        grid=(1,),
        in_specs=[blk, pl.BlockSpec((1, d), lambda i: (0, 0))],
        out_specs=blk,
        out_shape=jax.ShapeDtypeStruct((r, d), F32),
        compiler_params=_params(("arbitrary",)),
        name="rms_rows",
    )(x, g.reshape(1, d))


def _dwconv_kernel(x_ref, w_ref, b_ref, o_ref, buf_ref, *, width):
    t_len = x_ref.shape[1]
    lanes = x_ref.shape[2]
    lead = 33 - width
    buf_ref[pl.ds(0, 32), :] = jnp.zeros((32, lanes), F32)
    buf_ref[pl.ds(32, t_len), :] = x_ref[0]
    bias = b_ref[...]

    def chunk(c, carry):
        t0 = pl.multiple_of(c * CONV_CHUNK, 8)
        win = buf_ref[pl.ds(t0, CONV_CHUNK + 32), :]
        acc = jnp.broadcast_to(bias, (CONV_CHUNK, lanes))
        rows = CONV_CHUNK + 32
        for r in range(8):
            shift = r + lead
            z = win if shift == 0 else pltpu.roll(win, rows - shift, axis=0)
            for a, j in enumerate(range(r, width, 8)):
                acc = acc + w_ref[pl.ds(j, 1), :] * z[8 * a:8 * a + CONV_CHUNK]
        o_ref[0, pl.ds(t0, CONV_CHUNK), :] = acc
        return carry

    lax.fori_loop(0, t_len // CONV_CHUNK, chunk, 0)


def _dwconv(glu3, dw_w, dw_b):
    nb, t_len, d = glu3.shape
    width = dw_w.shape[0]
    blk = pl.BlockSpec((1, t_len, CONV_LANES), lambda b, c: (b, 0, c))
    return pl.pallas_call(
        functools.partial(_dwconv_kernel, width=width),
        grid=(nb, d // CONV_LANES),
        in_specs=[
            blk,
            pl.BlockSpec((width, CONV_LANES), lambda b, c: (0, c)),
            pl.BlockSpec((1, CONV_LANES), lambda b, c: (0, c)),
        ],
        out_specs=blk,
        out_shape=jax.ShapeDtypeStruct(glu3.shape, F32),
        scratch_shapes=[pltpu.VMEM((t_len + 32, CONV_LANES), F32)],
        compiler_params=_params(("parallel", "parallel")),
        name="dwconv",
    )(glu3, dw_w, dw_b.reshape(1, d))


def _dwconv_step_kernel(s_ref, x_ref, w_ref, b_ref, y_ref, ns_ref, buf_ref, *, width):
    n_state = width - 1
    n_new = x_ref.shape[1]
    d = x_ref.shape[2]
    buf_ref[pl.ds(0, n_state), :] = s_ref[0]
    buf_ref[pl.ds(n_state, n_new), :] = x_ref[0]
    acc = jnp.broadcast_to(b_ref[...], (n_new, d))
    for j in range(width):
        acc = acc + w_ref[pl.ds(j, 1), :] * buf_ref[pl.ds(j, n_new), :]
    y_ref[0] = acc
    ns_ref[0] = buf_ref[pl.ds(n_new, n_state), :]


def _dwconv_step(state, glu_s, dw_w, dw_b):
    nb, n_state, d = state.shape
    n_new = glu_s.shape[1]
    width = dw_w.shape[0]
    return pl.pallas_call(
        functools.partial(_dwconv_step_kernel, width=width),
        grid=(nb,),
        in_specs=[
            pl.BlockSpec((1, n_state, d), lambda b: (b, 0, 0)),
            pl.BlockSpec((1, n_new, d), lambda b: (b, 0, 0)),
            pl.BlockSpec((width, d), lambda b: (0, 0)),
            pl.BlockSpec((1, d), lambda b: (0, 0)),
        ],
        out_specs=[
            pl.BlockSpec((1, n_new, d), lambda b: (b, 0, 0)),
            pl.BlockSpec((1, n_state, d), lambda b: (b, 0, 0)),
        ],
        out_shape=[
            jax.ShapeDtypeStruct((nb, n_new, d), F32),
            jax.ShapeDtypeStruct((nb, n_state, d), F32),
        ],
        scratch_shapes=[pltpu.VMEM((n_state + n_new + 2, d), F32)],
        compiler_params=_params(("parallel",)),
        name="dwconv_step",
    )(state, glu_s, dw_w, dw_b.reshape(1, d))


def _neg_cumsum_kernel(lf_ref, o_ref, carry_ref):
    @pl.when(pl.program_id(1) == 0)
    def _():
        carry_ref[...] = jnp.zeros_like(carry_ref)

    row = lax.broadcasted_iota(jnp.int32, (LANES, LANES), 0)
    col = lax.broadcasted_iota(jnp.int32, (LANES, LANES), 1)
    upper = (row <= col).astype(BF16)
    hi, mid, lo = _split3(lf_ref[...])
    cs = (_dot(lo, upper) + _dot(mid, upper)) + _dot(hi, upper) + carry_ref[...]
    o_ref[...] = cs * -LOG2E
    carry_ref[...] = jnp.broadcast_to(cs[:, LANES - 1:LANES], cs.shape)


def _neg_cumsum(lft, nb):
    nh, m = lft.shape
    steps = m // nb // LANES
    blk = pl.BlockSpec((nh, LANES), lambda b, j: (0, b * steps + j))
    return pl.pallas_call(
        _neg_cumsum_kernel,
        grid=(nb, steps),
        in_specs=[blk],
        out_specs=blk,
        out_shape=jax.ShapeDtypeStruct((nh, m), F32),
        scratch_shapes=[pltpu.VMEM((nh, LANES), F32)],
        compiler_params=_params(("arbitrary", "arbitrary")),
        name="neg_cumsum",
    )(lft)


def _softmax_step(s, m_i, l_i, acc, v):
    m_new = jnp.maximum(m_i, jnp.max(s, axis=-1, keepdims=True))
    p = jnp.exp2(s - m_new)
    alpha = jnp.exp2(m_i - m_new)
    l_new = alpha * l_i + jnp.sum(p, axis=-1, keepdims=True)
    acc_new = alpha * acc + _dot(p.astype(BF16), v)
    return m_new, l_new, acc_new


def _attn_kernel(q_ref, k_ref, v_ref, nc_ref, o_ref, s_ref):
    qi = pl.program_id(2)
    tq = q_ref.shape[0]
    dh = q_ref.shape[1] // ATT_HEADS
    win = 2 * ATT_TILE
    heads = range(ATT_HEADS)

    def lanes(g):
        return slice(g * dh, (g + 1) * dh)

    def window(kj):
        return pl.ds(pl.multiple_of(kj * ATT_TILE, ATT_TILE), win)

    q = [q_ref[:, lanes(g)] for g in heads]

    def scores(kj):
        k = k_ref[window(kj), :]
        return [_dot_nt(q[g], k[:, lanes(g)]) + jnp.concatenate([nc_ref[g, kj], nc_ref[g, kj + 1]], axis=1)
                for g in heads]

    def store_scores(kj):
        for g, s in enumerate(scores(kj)):
            s_ref[g] = s

    full = lax.shift_right_logical(qi, 1)
    last = jnp.maximum(qi - 1, 0)
    store_scores(jnp.where(full > 0, 0, last))

    def body(p, carry):
        s = [s_ref[g] for g in heads]
        store_scores(jnp.where(p + 1 < full, 2 * (p + 1), last))
        v = v_ref[window(2 * p), :]
        return tuple(_softmax_step(s[g], *carry[g], v[:, lanes(g)]) for g in heads)

    init = (jnp.full((tq, 1), NEG_INF, F32), jnp.zeros((tq, 1), F32), jnp.zeros((tq, dh), F32))
    carry = lax.fori_loop(0, full, body, (init,) * ATT_HEADS)

    row = lax.broadcasted_iota(jnp.int32, (tq, win), 0) + qi * ATT_TILE
    col = lax.broadcasted_iota(jnp.int32, (tq, win), 1) + last * ATT_TILE
    keep = (col <= row) & (col >= 2 * full * ATT_TILE)
    v = v_ref[window(last), :]
    for g in heads:
        _, l_i, acc = _softmax_step(jnp.where(keep, s_ref[g], NEG_INF), *carry[g], v[:, lanes(g)])
        o_ref[:, lanes(g)] = (acc / l_i).astype(o_ref.dtype)


def _attention(q, k, v, negc, nb, nh):
    m, hd = q.shape
    dh = hd // nh
    t = m // nb
    nt = t // ATT_TILE
    assert t % ATT_TILE == 0 and nt >= 2 and nh % ATT_HEADS == 0
    negc4 = negc.reshape(nh, nb, nt, 1, ATT_TILE)
    gw = ATT_HEADS * dh
    return pl.pallas_call(
        _attn_kernel,
        grid=(nb, nh // ATT_HEADS, nt),
        in_specs=[
            pl.BlockSpec((ATT_TILE, gw), lambda b, h, i: (b * nt + i, h)),
            pl.BlockSpec((t, gw), lambda b, h, i: (b, h)),
            pl.BlockSpec((t, gw), lambda b, h, i: (b, h)),
            pl.BlockSpec((ATT_HEADS, None, nt, 1, ATT_TILE), lambda b, h, i: (h, b, 0, 0, 0)),
        ],
        out_specs=pl.BlockSpec((ATT_TILE, gw), lambda b, h, i: (b * nt + i, h)),
        out_shape=jax.ShapeDtypeStruct((m, hd), BF16),
        scratch_shapes=[pltpu.VMEM((ATT_HEADS, ATT_TILE, 2 * ATT_TILE), F32)],
        compiler_params=_params(("parallel", "parallel", "arbitrary")),
        name="fox_prompt",
    )(q, k, v, negc4)


def _sample_bias_kernel(pt_ref, lf_ref, lfn_ref, o_ref, g_ref, low_ref, *, n_pages, nh):
    b = pl.program_id(0)
    rpp = lfn_ref.shape[1]
    n_rows = g_ref.shape[0]

    @pl.when(b == 0)
    def _():
        r = lax.broadcasted_iota(jnp.int32, (n_rows, n_rows), 0)
        c = lax.broadcasted_iota(jnp.int32, (n_rows, n_rows), 1)
        low_ref[...] = (c < r).astype(BF16)
        g_ref[...] = jnp.zeros_like(g_ref)

    def gather(p, carry):
        src = pl.multiple_of(pt_ref[b * n_pages + p] * rpp, rpp)
        dst = pl.multiple_of(p * rpp, rpp)
        g_ref[pl.ds(dst, rpp), :] = lf_ref[pl.ds(src, rpp), :]
        return carry

    lax.fori_loop(0, n_pages, gather, 0)
    g_ref[pl.ds(n_pages * rpp, rpp), :] = lfn_ref[0]

    li = lax.broadcasted_iota(jnp.int32, (BIAS_LANES, BIAS_LANES), 0)
    lj = lax.broadcasted_iota(jnp.int32, (BIAS_LANES, BIAS_LANES), 1)
    same_head = (li & (nh - 1)) == (lj & (nh - 1))
    t_row = same_head.astype(BF16)
    t_pre = (same_head & (li <= lj)).astype(BF16)
    hi, mid, lo = _split3(g_ref[...])
    within = (_dot(lo, t_pre) + _dot(mid, t_pre)) + _dot(hi, t_pre)
    row_tot = (_dot(lo, t_row) + _dot(mid, t_row)) + _dot(hi, t_row)
    hi, mid, lo = _split3(row_tot)
    low = low_ref[...]
    before = (_dot(low, lo) + _dot(low, mid)) + _dot(low, hi)
    o_ref[0] = (within + before) * -LOG2E


def _sample_bias(page_table, lf_tab, lf_new, nh):
    ns, n_pages = page_table.shape
    rpp = lf_new.shape[1]
    n_rows = -(-((n_pages + 1) * rpp) // 16) * 16
    grid_spec = pltpu.PrefetchScalarGridSpec(
        num_scalar_prefetch=1,
        grid=(ns,),
        in_specs=[
            _single(lf_tab.shape, lambda b, pt: (0, 0)),
            pl.BlockSpec((1, rpp, BIAS_LANES), lambda b, pt: (b, 0, 0)),
        ],
        out_specs=pl.BlockSpec((1, n_rows, BIAS_LANES), lambda b, pt: (b, 0, 0)),
        scratch_shapes=[pltpu.VMEM((n_rows, BIAS_LANES), F32), pltpu.VMEM((n_rows, n_rows), BF16)],
    )
    return pl.pallas_call(
        functools.partial(_sample_bias_kernel, n_pages=n_pages, nh=nh),
        grid_spec=grid_spec,
        out_shape=jax.ShapeDtypeStruct((ns, n_rows, BIAS_LANES), F32),
        compiler_params=_params(("arbitrary",)),
        name="sample_bias",
    )(page_table.reshape(-1), lf_tab, lf_new)


def _paged_attn_kernel(pt_ref, q_ref, *refs, nh, n_new):
    pps = PAGES_PER_STEP
    k_refs, v_refs = refs[:pps], refs[pps:2 * pps]
    nc_ref, kn_ref, vn_ref, o_ref, qm_ref, mask_ref, m_ref, l_ref, acc_ref = refs[2 * pps:]
    p = pl.program_id(1)
    n_steps = pl.num_programs(1)
    rows, dh = qm_ref.shape
    cols = kn_ref.shape[0]
    rpp = cols // BIAS_LANES
    q_shift = n_new.bit_length() - 1
    h_shift = nh.bit_length() - 1

    @pl.when(p == 0)
    def _():
        q = q_ref[0].astype(F32)
        qm_ref[...] = jnp.concatenate([q[:, h * dh:(h + 1) * dh] for h in range(nh)], axis=0).astype(BF16)
        r = lax.broadcasted_iota(jnp.int32, (rows, cols), 0)
        c = lax.broadcasted_iota(jnp.int32, (rows, cols), 1)
        mask_ref[...] = jnp.where((r >> q_shift) == (c & (nh - 1)), 0.0, NEG_INF)
        m_ref[...] = jnp.full_like(m_ref, NEG_INF)
        l_ref[...] = jnp.zeros_like(l_ref)
        acc_ref[...] = jnp.zeros_like(acc_ref)

    def update(ks, vs, bias_row0, causal):
        qm = qm_ref[...]
        logits = [_dot_nt(qm, k) for k in ks]
        state = (m_ref[...], l_ref[...], acc_ref[...])
        for j, (s, v) in enumerate(zip(logits, vs)):
            bias = nc_ref[0, pl.ds(bias_row0 + j * rpp, rpp), :]
            s = jnp.concatenate([s[:, i * BIAS_LANES:(i + 1) * BIAS_LANES] + bias[i:i + 1, :]
                                 for i in range(rpp)], axis=1) + mask_ref[...]
            if causal:
                qpos = lax.broadcasted_iota(jnp.int32, s.shape, 0) & (n_new - 1)
                kpos = lax.broadcasted_iota(jnp.int32, s.shape, 1) >> h_shift
                s = jnp.where(kpos <= qpos, s, NEG_INF)
            state = _softmax_step(s, *state, v)
        m_ref[...], l_ref[...], acc_ref[...] = state

    update([r[...].astype(BF16) for r in k_refs], [r[...].astype(BF16) for r in v_refs],
           pl.multiple_of(p * (pps * rpp), rpp), False)

    @pl.when(p == n_steps - 1)
    def _():
        update([kn_ref[...]], [vn_ref[...]], n_steps * pps * rpp, True)
        o = acc_ref[...] / l_ref[...]
        for h in range(nh):
            o_ref[0, :, h * dh:(h + 1) * dh] = o[h * n_new:(h + 1) * n_new, :]


def _paged_attention(page_table, q_s, k_rows, v_rows, negc, k_new, v_new, nh):
    ns, n_new, hd = q_s.shape
    dh = hd // nh
    n_pages = page_table.shape[1]
    cols = k_new.shape[0] // ns
    rows = nh * n_new
    pps = PAGES_PER_STEP
    assert n_new & (n_new - 1) == 0 and nh & (nh - 1) == 0 and cols % BIAS_LANES == 0
    assert n_pages % pps == 0

    def cache(j):
        return pl.BlockSpec((cols, dh), lambda b, p, pt: (pt[b * n_pages + p * pps + j], 0))

    caches = [cache(j) for j in range(pps)]
    new = pl.BlockSpec((cols, dh), lambda b, p, pt: (b, 0))
    per_b = lambda shape: pl.BlockSpec(shape, lambda b, p, pt: (b, 0, 0))
    grid_spec = pltpu.PrefetchScalarGridSpec(
        num_scalar_prefetch=1,
        grid=(ns, n_pages // pps),
        in_specs=[per_b((1, n_new, hd))] + caches + caches + [per_b((1,) + negc.shape[1:]), new, new],
        out_specs=per_b((1, n_new, hd)),
        scratch_shapes=[
            pltpu.VMEM((rows, dh), BF16),
            pltpu.VMEM((rows, cols), F32),
            pltpu.VMEM((rows, 1), F32),
            pltpu.VMEM((rows, 1), F32),
            pltpu.VMEM((rows, dh), F32),
        ],
    )
    return pl.pallas_call(
        functools.partial(_paged_attn_kernel, nh=nh, n_new=n_new),
        grid_spec=grid_spec,
        out_shape=jax.ShapeDtypeStruct((ns, n_new, hd), F32),
        compiler_params=_params(("parallel", "arbitrary")),
        name="fox_sample",
    )(page_table.reshape(-1), q_s, *([k_rows] * pps), *([v_rows] * pps), negc, k_new, v_new)


def kernel(x_prompt, x_sample, cache_k, cache_v, cache_logf, state_conv, page_table, meta_tokens, norm_ffn1, ffn1_w1, ffn1_w3, ffn1_w2, norm_mix, conv_w_in, conv_b_in, conv_dw_w, conv_dw_b, conv_ln_g, conv_ln_b, conv_w_out, conv_b_out, norm_kv, w_k, w_v, w_f, b_f, w_q, w_o, norm_ffn2, ffn2_w1, ffn2_w3, ffn2_w2, norm_final):
    nb, seq, d = x_prompt.shape
    ns, s_len, _ = x_sample.shape
    n_meta = meta_tokens.shape[0]
    nh = w_f.shape[1]
    dh = w_k.shape[1] // nh
    n_pool, page = cache_k.shape[0], cache_k.shape[1]
    width = conv_dw_w.shape[1]
    depth = norm_ffn1.shape[0]
    n_a = conv_w_in.shape[0]

    tp = seq + n_meta
    tpad = -(-(tp + ns * s_len) // ATT_TILE) * ATT_TILE
    m = nb * tpad
    s0 = (nb - 1) * tpad + tp
    n_s = ns * s_len
    assert m % ROW_TILE == 0 and m % SQ_ROW_TILE == 0
    assert tpad % (KV_TILES_PER_SEQ * LANES) == 0 and tpad % CONV_CHUNK == 0
    assert width <= 33 and d % CONV_LANES == 0 and seq % OUT_TILE == 0 and n_meta % 8 == 0
    assert (page * nh) % BIAS_LANES == 0 and BIAS_LANES % nh == 0
    assert s0 % 16 == 0 and tpad - tp >= n_s and depth == 2 * n_a == 2

    meta = meta_tokens.astype(F32)
    pieces = []
    for b in range(nb):
        pieces += [meta, x_prompt[b]]
        if b < nb - 1:
            pieces.append(jnp.zeros((tpad - tp, d), F32))
    pieces += [x_sample.reshape(n_s, d), jnp.zeros((tpad - tp - n_s, d), F32)]
    h = jnp.concatenate(pieces, axis=0)

    def prompt_rows(x):
        return x.reshape(nb, tpad, x.shape[-1])[:, :tp]

    def sample_rows(x):
        return x[s0:s0 + n_s]

    h = _ffn(h, norm_ffn1, ffn1_w1, ffn1_w3, ffn1_w2, 0)
    glu = _rms_glu(h, norm_mix[0], conv_w_in[0], conv_b_in[0])
    y = _dwconv(glu.reshape(nb, tpad, d), conv_dw_w[0], conv_dw_b[0]).reshape(m, d)
    y_s, conv_s = _dwconv_step(state_conv[0], sample_rows(glu).reshape(ns, s_len, d),
                               conv_dw_w[0], conv_dw_b[0])
    y = lax.dynamic_update_slice(y, y_s.reshape(n_s, d), (s0, 0))
    conv_prompt = prompt_rows(glu)[None, :, tp - (width - 1):]
    conv_sample = conv_s[None]
    h = _ln_proj_res(y, conv_ln_g[0], conv_ln_b[0], conv_w_out[0], conv_b_out[0], h)
    h = _ffn(h, norm_ffn2, ffn2_w1, ffn2_w3, ffn2_w2, 0)

    k_prompt, v_prompt, k16, v16, lft = _kv(h, norm_kv, w_k, w_v, w_f, b_f, nb, tp)
    k_s, v_s = _kv_rows(sample_rows(h), norm_kv, w_k, w_v)
    h = _ffn(h, norm_ffn1, ffn1_w1, ffn1_w3, ffn1_w2, 1)
    q = _rms_proj(h, norm_mix[1], w_q[0], dh ** -0.5 * LOG2E, BF16)
    negc = _neg_cumsum(lft, nb)
    o = _attention(q, k16, v16, negc, nb, nh)

    def new_page(x):
        x = x.reshape(ns, s_len, x.shape[-1])
        return jnp.pad(x, ((0, 0), (0, page - s_len), (0, 0)))

    lf_rows = lft.T
    rpp = page * nh // BIAS_LANES
    negc_s = _sample_bias(page_table, cache_logf.reshape(n_pool * rpp, BIAS_LANES),
                          new_page(sample_rows(lf_rows)).reshape(ns, rpp, BIAS_LANES), nh)
    o_s = _paged_attention(
        page_table, sample_rows(q).reshape(ns, s_len, nh * dh),
        cache_k.reshape(n_pool * page * nh, dh), cache_v.reshape(n_pool * page * nh, dh), negc_s,
        new_page(sample_rows(k16)).reshape(ns * page * nh, dh),
        new_page(sample_rows(v16)).reshape(ns * page * nh, dh), nh)
    o = lax.dynamic_update_slice(o, o_s.reshape(n_s, nh * dh).astype(BF16), (s0, 0))
    h = _proj_res(o, w_o[0], h)
    h = _ffn(h, norm_ffn2, ffn2_w1, ffn2_w3, ffn2_w2, 1)

    y_prompt = _rms_out(h, norm_final, nb, seq, tpad, n_meta)
    y_sample = _rms_rows(sample_rows(h), norm_final).reshape(ns, s_len, d)
    k_prompt = k_prompt.reshape(nb, tp, nh, dh)
    v_prompt = v_prompt.reshape(nb, tp, nh, dh)
    logf_prompt = prompt_rows(lf_rows)
    k_sample = k_s.reshape(ns, s_len, nh, dh)
    v_sample = v_s.reshape(ns, s_len, nh, dh)
    logf_sample = sample_rows(lf_rows).reshape(ns, s_len, nh)
    return (y_prompt, y_sample, k_prompt, v_prompt, logf_prompt, conv_prompt,
            k_sample, v_sample, logf_sample, conv_sample)
```
